```python
import math
import jax, jax.numpy as jnp
from jax import lax
import numpy as np

D_MODEL = 2048
BATCH = 2
SEQ = 8192
DEPTH = 4

N_EVEN = (DEPTH + 1) // 2
N_ODD = DEPTH // 2

A_HEAD = 64
A_HEADS = D_MODEL // 2 // A_HEAD
A_W = A_HEADS * A_HEAD
DECAY_LORA = 64
AAA_LORA = 64
MV_LORA = 32
LN_X_EPS = A_HEAD * 1e-5
B_HEAD = 64
B_HQ = D_MODEL // 2 // B_HEAD
B_HKV = 4
B_GROUP = B_HQ // B_HKV
B_W = B_HQ * B_HEAD
B_KVW = B_HKV * B_HEAD
WINDOW = 128
BLOCK = 128
ROPE_DIM = B_HEAD // 4
ROPE_THETA = 500000.0
C_CHUNK = 128
C_GROUPS = 16
C_W = D_MODEL
C_GW = C_W // C_GROUPS
RMS_EPS = 1e-5
LN_EPS = 1e-5

SHIFT_W = 3 * A_W + DECAY_LORA + AAA_LORA
E_COLS = SHIFT_W + A_W + B_W + 2 * B_KVW + B_W
O_COLS = 3 * C_W

kernel_name = 'hybrid_rwkv7_swa_sinks_sgu'


def rmsnorm(x, g):
    xf = x.astype(jnp.float32)
    y = xf * lax.rsqrt(jnp.mean(xf * xf, axis=-1, keepdims=True) + RMS_EPS)
    return (y * g.astype(jnp.float32)).astype(x.dtype)


def layernorm(x, w, b):
    xf = x.astype(jnp.float32)
    mean = jnp.mean(xf, axis=-1, keepdims=True)
    var = jnp.mean(jnp.square(xf - mean), axis=-1, keepdims=True)
    y = (xf - mean) * lax.rsqrt(var + LN_EPS)
    return (y * w.astype(jnp.float32) + b.astype(jnp.float32)).astype(x.dtype)


def token_shift(z, mu):
    prev = jnp.pad(z[:, :-1], ((0, 0), (1, 0), (0, 0)))
    return z + (prev - z) * mu


def wkv7_scan(r, w, k, v, a, b):
    f32 = jnp.float32
    xs = tuple(jnp.moveaxis(t.astype(f32), 1, 0) for t in (r, w, k, v, a, b))
    B_, _, H, N = r.shape
    s0 = jnp.zeros((B_, H, N, N), f32)

    def step(S, inp):
        r_t, w_t, k_t, v_t, a_t, b_t = inp
        sa = jnp.einsum('bhvk,bhk->bhv', S, a_t)
        S = S * w_t[:, :, None, :] + sa[..., None] * b_t[:, :, None, :] + v_t[..., None] * k_t[:, :, None, :]
        y = jnp.einsum('bhvk,bhk->bhv', S, r_t)
        return S, y

    _, ys = lax.scan(step, s0, xs)
    return jnp.moveaxis(ys, 0, 1)


def rwkv7_mix(z, mu, w0, w2, a0, a2, k_k, k_a, r_k, ln_w, ln_b, v_first, v0=None, v1=None, v2=None):
    B_, T, _ = z.shape
    z = token_shift(z, mu)
    r, k, v, xw, xa = jnp.split(z, [A_W, 2 * A_W, 3 * A_W, 3 * A_W + DECAY_LORA], axis=-1)
    w_log = -jax.nn.softplus(-(w0 + jnp.tanh(xw) @ w2)) - 0.5
    decay = jnp.exp(-jnp.exp(w_log.astype(jnp.float32)))
    a = jax.nn.sigmoid(a0 + xa @ a2)
    if v_first is None:
        v_first = v
    else:
        v = v + (v_first - v) * jax.nn.sigmoid(v0 + (v @ v1) @ v2)
    heads = lambda t: t.reshape(B_, T, A_HEADS, A_HEAD)
    kk = heads(k * k_k).astype(jnp.float32)
    kk = kk * lax.rsqrt(jnp.sum(kk * kk, axis=-1, keepdims=True) + 1e-12)
    k = k * (1 + (a - 1) * k_a)
    r_h, k_h, v_h, a_h = heads(r), heads(k), heads(v), heads(a)
    y = wkv7_scan(r_h, heads(decay), k_h, v_h, -kk, kk * a_h.astype(jnp.float32))
    mean = jnp.mean(y, axis=-1, keepdims=True)
    var = jnp.mean(jnp.square(y - mean), axis=-1, keepdims=True)
    y = ((y - mean) * lax.rsqrt(var + LN_X_EPS)).reshape(B_, T, A_W)
    y = y * ln_w.astype(jnp.float32) + ln_b.astype(jnp.float32)
    bonus = jnp.sum(r_h * k_h * r_k, axis=-1, keepdims=True) * v_h
    y = y + bonus.reshape(B_, T, A_W).astype(jnp.float32)
    return y.astype(z.dtype), v_first


def rope_partial(x, positions):
    half = ROPE_DIM // 2
    inv_freq = jnp.power(jnp.float32(ROPE_THETA), -jnp.arange(half, dtype=jnp.float32) / half)
    ang = positions.astype(jnp.float32)[..., None] * inv_freq
    cos = jnp.cos(ang)[:, :, None, :]
    sin = jnp.sin(ang)[:, :, None, :]
    xr = x[..., :ROPE_DIM].astype(jnp.float32)
    x1, x2 = xr[..., :half], xr[..., half:]
    rot = jnp.concatenate([x1 * cos - x2 * sin, x2 * cos + x1 * sin], axis=-1)
    return jnp.concatenate([rot.astype(x.dtype), x[..., ROPE_DIM:]], axis=-1)


def swa_sinks(q, k, v, sinks):
    B_, T, _, _ = q.shape
    nb = T // BLOCK
    qb = q.reshape(B_, nb, BLOCK, B_HKV, B_GROUP, B_HEAD)
    kb = k.reshape(B_, nb, BLOCK, B_HKV, B_HEAD)
    vb = v.reshape(B_, nb, BLOCK, B_HKV, B_HEAD)

    def with_prev(t):
        prev = jnp.pad(t[:, :-1], ((0, 0), (1, 0), (0, 0), (0, 0), (0, 0)))
        return jnp.concatenate([prev, t], axis=2)

    kw, vw = with_prev(kb), with_prev(vb)
    scale = 1.0 / math.sqrt(B_HEAD)
    s = jnp.einsum('bnqhgd,bnkhd->bnhgqk', qb, kw, preferred_element_type=jnp.float32) * scale
    qi = jnp.arange(BLOCK)[:, None] + BLOCK
    ki = jnp.arange(2 * BLOCK)[None, :]
    band = (ki <= qi) & (qi - ki < WINDOW)
    has_prev = (jnp.arange(nb)[:, None, None] > 0) | (ki >= BLOCK)[None]
    valid = band[None] & has_prev
    s = jnp.where(valid[None, :, None, None], s, -jnp.inf)
    sink = sinks.astype(jnp.float32).reshape(B_HKV, B_GROUP)[None, None, :, :, None, None]
    m = jnp.maximum(jnp.max(s, axis=-1, keepdims=True), sink)
    p = jnp.exp(s - m)
    p = p / (jnp.sum(p, axis=-1, keepdims=True) + jnp.exp(sink - m))
    o = jnp.einsum('bnhgqk,bnkhd->bnqhgd', p.astype(v.dtype), vw)
    return o.reshape(B_, T, B_W)


def chunked_sgu(u, v, ln_w, ln_b, ws, bs):
    B_, T, _ = u.shape
    nc = T // C_CHUNK
    v = layernorm(v, ln_w, ln_b)
    vc = v.reshape(B_, nc, C_CHUNK, C_GROUPS, C_GW)
    causal = jnp.tril(jnp.ones((C_CHUNK, C_CHUNK), dtype=bool))
    wm = jnp.where(causal[None], ws, jnp.zeros_like(ws))
    mixed = jnp.einsum('gts,bnsgc->bntgc', wm, vc) + bs.T[None, None, :, :, None]
    return u * mixed.reshape(B_, T, C_W)


def setup_inputs(seed: int = 0) -> dict:
    key = jax.random.key(seed)
    ks = iter(jax.random.split(key, 32))
    nrm = lambda shape, s: jax.random.normal(next(ks), shape, jnp.float32) * s
    NE, NO = N_EVEN, N_ODD
    return {
        'x': nrm((BATCH, SEQ, D_MODEL), 1.0),
        'positions': (jax.random.randint(next(ks), (BATCH, 1), 0, 4096) + jnp.arange(SEQ)[None, :]).astype(jnp.int32),
        'e_norm': 1.0 + nrm((NE, D_MODEL), 0.02),
        'e_w_in': nrm((NE, D_MODEL, E_COLS), D_MODEL ** -0.5),
        'e_mu': jax.random.uniform(next(ks), (NE, SHIFT_W), jnp.float32),
        'rwkv_w0': jax.random.uniform(next(ks), (NE, A_W), jnp.float32, -6.0, -1.0),
        'rwkv_w2': nrm((NE, DECAY_LORA, A_W), 0.1 * DECAY_LORA ** -0.5),
        'rwkv_a0': nrm((NE, A_W), 0.1),
        'rwkv_a2': nrm((NE, AAA_LORA, A_W), 0.1 * AAA_LORA ** -0.5),
        'rwkv_k_k': 1.0 + nrm((NE, A_W), 0.1),
        'rwkv_k_a': 1.0 + nrm((NE, A_W), 0.1),
        'rwkv_r_k': nrm((NE, A_HEADS, A_HEAD), 0.1),
        'rwkv_ln_w': 1.0 + nrm((NE, A_W), 0.02),
        'rwkv_ln_b': nrm((NE, A_W), 0.01),
        'rwkv_v0': nrm((NE - 1, A_W), 0.1),
        'rwkv_v1': nrm((NE - 1, A_W, MV_LORA), A_W ** -0.5),
        'rwkv_v2': nrm((NE - 1, MV_LORA, A_W), 0.1 * MV_LORA ** -0.5),
        'attn_sinks': nrm((NE, B_HQ), 0.5),
        'e_w_out': nrm((NE, A_W + B_W, D_MODEL), (A_W + B_W) ** -0.5),
        'o_norm': 1.0 + nrm((NO, D_MODEL), 0.02),
        'o_w_in': nrm((NO, D_MODEL, O_COLS), D_MODEL ** -0.5),
        'sgu_ln_w': 1.0 + nrm((NO, C_W), 0.02),
        'sgu_ln_b': nrm((NO, C_W), 0.01),
        'sgu_ws': nrm((NO, C_GROUPS, C_CHUNK, C_CHUNK), C_CHUNK ** -0.5),
        'sgu_bs': 1.0 + nrm((NO, C_GROUPS, C_CHUNK), 0.1),
        'o_w_out': nrm((NO, C_W, D_MODEL), C_W ** -0.5),
        'final_norm': 1.0 + nrm((D_MODEL,), 0.02),
    }


def reference(x, positions, e_norm, e_w_in, e_mu, rwkv_w0, rwkv_w2, rwkv_a0, rwkv_a2, rwkv_k_k, rwkv_k_a,
              rwkv_r_k, rwkv_ln_w, rwkv_ln_b, rwkv_v0, rwkv_v1, rwkv_v2, attn_sinks, e_w_out,
              o_norm, o_w_in, sgu_ln_w, sgu_ln_b, sgu_ws, sgu_bs, o_w_out, final_norm):
    B_, T, _ = x.shape
    v_first = None
    for layer in range(DEPTH):
        if layer % 2 == 0:
            e = layer // 2
            h = rmsnorm(x, e_norm[e])
            z = h @ e_w_in[e]
            a_in, a_gate, q, kB, vB, b_gate = jnp.split(
                z, [SHIFT_W, SHIFT_W + A_W, SHIFT_W + A_W + B_W, SHIFT_W + A_W + B_W + B_KVW,
                    SHIFT_W + A_W + B_W + 2 * B_KVW], axis=-1)
            if e == 0:
                yA, v_first = rwkv7_mix(a_in, e_mu[e], rwkv_w0[e], rwkv_w2[e], rwkv_a0[e], rwkv_a2[e],
                                        rwkv_k_k[e], rwkv_k_a[e], rwkv_r_k[e], rwkv_ln_w[e], rwkv_ln_b[e], None)
            else:
                yA, v_first = rwkv7_mix(a_in, e_mu[e], rwkv_w0[e], rwkv_w2[e], rwkv_a0[e], rwkv_a2[e],
                                        rwkv_k_k[e], rwkv_k_a[e], rwkv_r_k[e], rwkv_ln_w[e], rwkv_ln_b[e],
                                        v_first, rwkv_v0[e - 1], rwkv_v1[e - 1], rwkv_v2[e - 1])
            qh = rope_partial(q.reshape(B_, T, B_HQ, B_HEAD), positions)
            kh = rope_partial(kB.reshape(B_, T, B_HKV, B_HEAD), positions)
            vh = vB.reshape(B_, T, B_HKV, B_HEAD)
            yB = swa_sinks(qh, kh, vh, attn_sinks[e])
            y = jnp.concatenate([yA * jax.nn.silu(a_gate), yB * jax.nn.silu(b_gate)], axis=-1)
            x = x + y @ e_w_out[e]
        else:
            o = layer // 2
            h = rmsnorm(x, o_norm[o])
            z = h @ o_w_in[o]
            u, vv, gate = jnp.split(z, [C_W, 2 * C_W], axis=-1)
            y = chunked_sgu(u, vv, sgu_ln_w[o], sgu_ln_b[o], sgu_ws[o], sgu_bs[o]) * jax.nn.silu(gate)
            x = x + y @ o_w_out[o]
    return rmsnorm(x, final_norm)
```

```python
import functools
import math

import jax
import jax.numpy as jnp
from jax import lax
from jax.experimental import pallas as pl
from jax.experimental.pallas import tpu as pltpu

F32 = jnp.float32
BF16 = jnp.bfloat16

D_MODEL = 2048
HEAD = 64
A_W = 1024
B_W = 1024
B_KVW = 256
B_GROUP = 4
LORA = 64
SHIFT_W = 3 * A_W + 2 * LORA
WINDOW = 128
ROPE_DIM = 16
ROPE_THETA = 500000.0
SGU_CHUNK = 128
SGU_GROUPS = 16
RMS_EPS = 1e-5
LN_EPS = 1e-5
LN_X_EPS = HEAD * 1e-5

LANES = 128
WKV_CHUNK = 64
VMEM_LIMIT = 56 * 1024 * 1024


def _cparams(n_axes):
    return pltpu.CompilerParams(
        dimension_semantics=("arbitrary",) * n_axes, vmem_limit_bytes=VMEM_LIMIT)


def _bdot(a, b):
    return jnp.dot(a.astype(BF16), b.astype(BF16), preferred_element_type=F32)


def _bdot_nt(a, b):
    return lax.dot_general(a.astype(BF16), b.astype(BF16), (((1,), (1,)), ((), ())),
                           preferred_element_type=F32)


def _bdot_tn(a, b):
    return lax.dot_general(a.astype(BF16), b.astype(BF16), (((0,), (0,)), ((), ())),
                           preferred_element_type=F32)


def _split_dot(x, ones_bf16, passes):
    acc = None
    rem = x
    for _ in range(passes):
        piece = rem.astype(BF16)
        term = jnp.dot(piece, ones_bf16, preferred_element_type=F32)
        acc = term if acc is None else acc + term
        rem = rem - piece.astype(F32)
    return acc


def _silu(x):
    return x * (1.0 / (1.0 + jnp.exp(-x)))


def _sigmoid(x):
    return 1.0 / (1.0 + jnp.exp(-x))


def _head_masks():
    lane = lax.broadcasted_iota(jnp.int32, (1, LANES), 1)
    return lane < HEAD, lane >= HEAD


def _pair_blockdiag(x, m0, m1):
    zero = jnp.zeros_like(x)
    return jnp.concatenate([jnp.where(m0, x, zero), jnp.where(m1, x, zero)], axis=0)


def _norm_matmul_kernel(x_ref, g_ref, w_ref, *out_refs, seg_widths, col_chunk):
    x = x_ref[...]
    ms = jnp.mean(x * x, axis=-1, keepdims=True)
    h = ((x * lax.rsqrt(ms + RMS_EPS)) * g_ref[...]).astype(BF16)
    col = 0
    for o_ref, width in zip(out_refs, seg_widths):
        for c0 in range(0, width, col_chunk):
            cw = min(col_chunk, width - c0)
            o_ref[:, c0:c0 + cw] = jnp.dot(
                h, w_ref[:, col + c0:col + c0 + cw], preferred_element_type=F32)
        col += width


def _norm_matmul(x2d, g, w_bf16, seg_widths, tm):
    n, d = x2d.shape
    ncol = w_bf16.shape[1]
    assert sum(seg_widths) == ncol and n % tm == 0
    kern = functools.partial(_norm_matmul_kernel, seg_widths=tuple(seg_widths), col_chunk=512)
    return pl.pallas_call(
        kern,
        grid=(n // tm,),
        in_specs=[
            pl.BlockSpec((tm, d), lambda i: (i, 0)),
            pl.BlockSpec((1, d), lambda i: (0, 0)),
            pl.BlockSpec((d, ncol), lambda i: (0, 0), pipeline_mode=pl.Buffered(1)),
        ],
        out_specs=[pl.BlockSpec((tm, w), lambda i: (i, 0)) for w in seg_widths],
        out_shape=[jax.ShapeDtypeStruct((n, w), F32) for w in seg_widths],
        compiler_params=_cparams(1),
        name="norm_matmul",
    )(x2d, g.reshape(1, d), w_bf16)


def _rwkv_prep_kernel(*refs, has_vres):
    if has_vres:
        (z_ref, zp_ref, mu_ref, w0_ref, a0_ref, wa2_ref, kk_ref, ka_ref, tri_ref, ones_ref,
         vf_ref, v0_ref, v1_ref, v2_ref,
         r_out, k_out, v_out, na_out, nb_out, lw_out, cs_out) = refs
    else:
        (z_ref, zp_ref, mu_ref, w0_ref, a0_ref, wa2_ref, kk_ref, ka_ref, tri_ref, ones_ref,
         r_out, k_out, v_out, na_out, nb_out, lw_out, cs_out) = refs
    i = pl.program_id(1)
    z = z_ref[0]
    tt = z.shape[0]
    prev_last = jnp.where(i > 0, zp_ref[0, 7:8, :], 0.0)
    rolled = pltpu.roll(z, 1, axis=0)
    row = lax.broadcasted_iota(jnp.int32, (tt, 1), 0)
    prev = jnp.where(row == 0, prev_last, rolled)
    zz = z + (prev - z) * mu_ref[...]
    r = zz[:, 0:A_W]
    k = zz[:, A_W:2 * A_W]
    v = zz[:, 2 * A_W:3 * A_W]
    xwa = zz[:, 3 * A_W:3 * A_W + 2 * LORA]
    lane = lax.broadcasted_iota(jnp.int32, (1, 2 * LORA), 1)
    lora_in = jnp.where(lane < LORA, jnp.tanh(xwa), xwa)
    lora = _bdot(lora_in, wa2_ref[...])
    wpre = w0_ref[...] + lora[:, 0:A_W]
    neg = -wpre
    softplus = jnp.maximum(neg, 0.0) + jnp.log(1.0 + jnp.exp(-jnp.abs(neg)))
    w_log = -softplus - 0.5
    lw = -jnp.exp(w_log)
    a = _sigmoid(a0_ref[...] + lora[:, A_W:2 * A_W])
    if has_vres:
        vf = vf_ref[0]
        gate = _sigmoid(v0_ref[...] + _bdot(_bdot(v, v1_ref[...]), v2_ref[...]))
        v = v + (vf - v) * gate
    kk = k * kk_ref[...]
    k2 = k * (1.0 + (a - 1.0) * ka_ref[...])
    ones_bd = ones_ref[...]
    tri = tri_ref[...]
    for p in range(A_W // LANES):
        sl = slice(p * LANES, (p + 1) * LANES)
        kkp = kk[:, sl]
        ss = _split_dot(kkp * kkp, ones_bd, 2)
        kkn = kkp * lax.rsqrt(ss + 1e-12)
        na_out[0, :, sl] = -kkn
        nb_out[0, :, sl] = kkn * a[:, sl]
    r_out[0] = r
    k_out[0] = k2
    v_out[0] = v
    lw_out[0] = lw
    acc = None
    rem = lw
    for _ in range(3):
        piece = rem.astype(BF16)
        term = jnp.dot(tri, piece, preferred_element_type=F32)
        acc = term if acc is None else acc + term
        rem = rem - piece.astype(F32)
    cs_out[0] = acc


def _rwkv_prep(a_in, mu, w0, w2, a0, a2, k_k, k_a, vres, tt):
    b, t, _ = a_in.shape
    has_vres = vres is not None
    wa2 = jnp.zeros((2 * LORA, 2 * A_W), F32)
    wa2 = wa2.at[:LORA, :A_W].set(w2).at[LORA:, A_W:].set(a2).astype(BF16)
    ti = jnp.arange(tt)
    tri = ((ti[:, None] >= ti[None, :]) &
           (ti[:, None] // WKV_CHUNK == ti[None, :] // WKV_CHUNK)).astype(BF16)
    li = jnp.arange(LANES)
    ones_bd = (li[:, None] // HEAD == li[None, :] // HEAD).astype(BF16)
    row = lambda p: p.reshape(1, -1)
    full = lambda shape: pl.BlockSpec(shape, lambda bi, i: (0,) * len(shape))
    tile = lambda w: pl.BlockSpec((1, tt, w), lambda bi, i: (bi, i, 0))
    in_specs = [
        tile(SHIFT_W),
        pl.BlockSpec((1, 8, SHIFT_W), lambda bi, i: (bi, jnp.maximum(i * (tt // 8) - 1, 0), 0)),
        full((1, SHIFT_W)), full((1, A_W)), full((1, A_W)), full((2 * LORA, 2 * A_W)),
        full((1, A_W)), full((1, A_W)), full((tt, tt)), full((LANES, LANES)),
    ]
    args = [a_in, a_in, row(mu), row(w0), row(a0), wa2, row(k_k), row(k_a), tri, ones_bd]
    if has_vres:
        v_first, v0, v1, v2 = vres
        in_specs += [tile(A_W), full((1, A_W)), full(v1.shape), full(v2.shape)]
        args += [v_first, row(v0), v1.astype(BF16), v2.astype(BF16)]
    return pl.pallas_call(
        functools.partial(_rwkv_prep_kernel, has_vres=has_vres),
        grid=(b, t // tt),
        in_specs=in_specs,
        out_specs=[tile(A_W)] * 7,
        out_shape=[jax.ShapeDtypeStruct((b, t, A_W), F32)] * 7,
        compiler_params=_cparams(2),
        name="rwkv_prep",
    )(*args)


def _wkv_kernel(r_ref, k_ref, v_ref, na_ref, nb_ref, lw_ref, cs_ref, g_ref,
                lnw_ref, lnb_ref, rk_ref, ones_ref, o_ref, s_ref):
    c = WKV_CHUNK

    @pl.when(pl.program_id(1) == 0)
    def _():
        s_ref[...] = jnp.zeros_like(s_ref)

    m0, m1 = _head_masks()
    ti = lax.broadcasted_iota(jnp.int32, (c, LANES), 0)
    tj = lax.broadcasted_iota(jnp.int32, (c, LANES), 1) & (c - 1)
    strict = ti > tj
    incl = ti >= tj
    eye_t = jnp.where(ti == tj, 1.0, 0.0)
    li = lax.broadcasted_iota(jnp.int32, (LANES, LANES), 0)
    lj = lax.broadcasted_iota(jnp.int32, (LANES, LANES), 1)
    bdmask = (li >= HEAD) == (lj >= HEAD)
    eye_full = li == lj
    ones_bd = ones_ref[...]
    bd = lambda x: _pair_blockdiag(x, m0, m1)

    for p in range(A_W // LANES):
        sl = slice(p * LANES, (p + 1) * LANES)
        rc = r_ref[0, :, sl]
        kc = k_ref[0, :, sl]
        vc = v_ref[0, :, sl]
        ac = na_ref[0, :, sl]
        bc = nb_ref[0, :, sl]
        lwc = lw_ref[0, :, sl]
        cs = cs_ref[0, :, sl]
        mid = cs[c // 2 - 1:c // 2, :]
        last = cs[c - 1:c, :]
        e_pos = jnp.exp(cs - mid)
        e_neg = jnp.exp(mid - cs)
        rt = rc * e_pos
        at = ac * jnp.exp(cs - lwc - mid)
        kt = kc * e_neg
        bt = bc * e_neg
        em = jnp.exp(mid)
        e_end = jnp.exp(last - mid)
        wc = jnp.exp(last)
        bh = bt * e_end
        kh = kt * e_end
        sc = _bdot_nt(jnp.concatenate([at, rt], axis=0), jnp.concatenate([bd(bt), bd(kt)], axis=0))
        a_ab = jnp.where(strict, sc[:c, :LANES], 0.0)
        a_ak = jnp.where(strict, sc[:c, LANES:], 0.0)
        a_rb = jnp.where(incl, sc[c:, :LANES], 0.0)
        a_rk = jnp.where(incl, sc[c:, LANES:], 0.0)
        apow = a_ab
        pinv = eye_t
        for _ in range(5):
            both = _bdot(jnp.concatenate([apow, pinv], axis=0), bd(apow))
            pinv = pinv + both[c:]
            apow = both[:c]
        pinv = pinv + _bdot(pinv, bd(apow))
        xv = _bdot(jnp.concatenate([a_ak, a_rk], axis=0), bd(vc))
        ta = _bdot(pinv, jnp.concatenate([bd(xv[:c]), bd(at * em)], axis=1))
        uv = ta[:, :LANES]
        ap = ta[:, LANES:]
        rb = _bdot(a_rb, jnp.concatenate([bd(ap), bd(uv)], axis=1))
        rp = rt * em + rb[:, :LANES]
        yv = rb[:, LANES:] + xv[c:]
        gram = _bdot_tn(jnp.concatenate([ap, uv, vc], axis=1), jnp.concatenate([bh, kh], axis=1))
        m_mat = jnp.where(eye_full, wc, 0.0) + jnp.where(bdmask, gram[:LANES, :LANES], 0.0)
        n_mat = jnp.where(bdmask, gram[LANES:2 * LANES, :LANES] + gram[2 * LANES:, LANES:], 0.0)
        s = s_ref[p]
        y = _bdot_nt(rp, s) + yv
        s_ref[p] = _bdot(s, m_mat) + n_mat
        stats = _split_dot(jnp.concatenate([y, rc * kc * rk_ref[:, sl]], axis=0), ones_bd, 2)
        mean = stats[:c] * (1.0 / HEAD)
        d = y - mean
        var = _split_dot(d * d, ones_bd, 2) * (1.0 / HEAD)
        yn = d * lax.rsqrt(var + LN_X_EPS) * lnw_ref[:, sl] + lnb_ref[:, sl]
        out = yn + stats[c:] * vc
        o_ref[0, :, sl] = out * _silu(g_ref[0, :, sl])


def _wkv(r, k, v, na, nb, lw, cs, gate, ln_w, ln_b, r_k):
    b, t, _ = r.shape
    c = WKV_CHUNK
    li = jnp.arange(LANES)
    ones_bd = (li[:, None] // HEAD == li[None, :] // HEAD).astype(BF16)
    tile = pl.BlockSpec((1, c, A_W), lambda bi, i: (bi, i, 0))
    vec = pl.BlockSpec((1, A_W), lambda bi, i: (0, 0))
    return pl.pallas_call(
        _wkv_kernel,
        grid=(b, t // c),
        in_specs=[tile] * 8 + [vec] * 3 + [pl.BlockSpec((LANES, LANES), lambda bi, i: (0, 0))],
        out_specs=tile,
        out_shape=jax.ShapeDtypeStruct((b, t, A_W), F32),
        scratch_shapes=[pltpu.VMEM((A_W // LANES, LANES, LANES), F32)],
        compiler_params=_cparams(2),
        name="wkv",
    )(r, k, v, na, nb, lw, cs, gate, ln_w.reshape(1, A_W), ln_b.reshape(1, A_W),
      r_k.reshape(1, A_W), ones_bd)


def _rope_tables(pos_col, freq, sgn):
    ang = pos_col.astype(F32) * freq
    return jnp.cos(ang), jnp.sin(ang) * sgn


def _rope(x, cos_t, sin_t, first8):
    fwd = pltpu.roll(x, LANES - ROPE_DIM // 2, axis=1)
    bwd = pltpu.roll(x, ROPE_DIM // 2, axis=1)
    return x * cos_t + jnp.where(first8, fwd, bwd) * sin_t


def _dup_head(tile, which, m0):
    swapped = pltpu.roll(tile, HEAD, axis=1)
    if which == 0:
        return jnp.where(m0, tile, swapped)
    return jnp.where(m0, swapped, tile)


def _swa_kernel(q_ref, kvc_ref, kvp_ref, posc_ref, posp_ref, g_ref, sink_ref, freq_ref, sgn_ref,
                o_ref):
    n = pl.program_id(1)
    blk = WINDOW
    m0, m1 = _head_masks()
    lane = lax.broadcasted_iota(jnp.int32, (1, LANES), 1)
    first8 = (lane & (HEAD - 1)) < ROPE_DIM // 2
    cos_c, sin_c = _rope_tables(posc_ref[0], freq_ref[...], sgn_ref[...])
    cos_p, sin_p = _rope_tables(posp_ref[0], freq_ref[...], sgn_ref[...])
    qi = lax.broadcasted_iota(jnp.int32, (blk, 2 * blk), 0) + blk
    ki = lax.broadcasted_iota(jnp.int32, (blk, 2 * blk), 1)
    valid = (ki <= qi) & (qi - ki < WINDOW) & ((n > 0) | (ki >= blk))
    scale = 1.0 / math.sqrt(HEAD)
    n_kv_tiles = B_KVW // LANES
    for kt in range(n_kv_tiles):
        k_tile = jnp.concatenate(
            [_rope(kvp_ref[0, :, kt * LANES:(kt + 1) * LANES], cos_p, sin_p, first8),
             _rope(kvc_ref[0, :, kt * LANES:(kt + 1) * LANES], cos_c, sin_c, first8)], axis=0)
        v_tile = jnp.concatenate(
            [kvp_ref[0, :, B_KVW + kt * LANES:B_KVW + (kt + 1) * LANES],
             kvc_ref[0, :, B_KVW + kt * LANES:B_KVW + (kt + 1) * LANES]], axis=0)
        for which in range(2):
            g = 2 * kt + which
            k2 = _dup_head(k_tile, which, m0).astype(BF16)
            v2 = _dup_head(v_tile, which, m0)
            v2_h0 = jnp.where(m0, v2, 0.0).astype(BF16)
            v2_h1 = jnp.where(m1, v2, 0.0).astype(BF16)
            for pr in range(B_GROUP // 2):
                qt = 2 * g + pr
                sl = slice(qt * LANES, (qt + 1) * LANES)
                qp = _rope(q_ref[0, :, sl], cos_c, sin_c, first8) * scale
                acc = None
                inv = None
                for half, (msk, vh) in enumerate(((m0, v2_h0), (m1, v2_h1))):
                    qh = jnp.where(msk, qp, 0.0)
                    s = _bdot_nt(qh, k2)
                    s = jnp.where(valid, s, -jnp.inf)
                    sink = sink_ref[:, qt * LANES + half * HEAD:qt * LANES + half * HEAD + 1]
                    m = jnp.maximum(jnp.max(s, axis=-1, keepdims=True), sink)
                    pexp = jnp.exp(s - m)
                    den = jnp.sum(pexp, axis=-1, keepdims=True) + jnp.exp(sink - m)
                    o = jnp.dot(pexp.astype(BF16), vh, preferred_element_type=F32)
                    acc = o if acc is None else acc + o
                    r_den = 1.0 / den
                    inv = r_den if inv is None else jnp.where(m0, inv, r_den)
                o_ref[0, :, sl] = acc * inv * _silu(g_ref[0, :, sl])


def _swa(q, kv, positions, gate, sinks):
    b, t, _ = q.shape
    blk = WINDOW
    half = ROPE_DIM // 2
    li = jnp.arange(LANES) % HEAD
    inv_freq = jnp.power(jnp.float32(ROPE_THETA), -jnp.arange(half, dtype=F32) / half)
    freq = jnp.where(li < ROPE_DIM, inv_freq[li % half], 0.0).reshape(1, LANES).astype(F32)
    sgn = jnp.where(li < half, -1.0, jnp.where(li < ROPE_DIM, 1.0, 0.0)).reshape(1, LANES).astype(F32)
    sink_lanes = jnp.repeat(sinks.astype(F32), HEAD).reshape(1, B_W)
    pos3 = positions.reshape(b, t, 1)
    cur = lambda w: pl.BlockSpec((1, blk, w), lambda bi, i: (bi, i, 0))
    prev = lambda w: pl.BlockSpec((1, blk, w), lambda bi, i: (bi, jnp.maximum(i - 1, 0), 0))
    full = lambda w: pl.BlockSpec((1, w), lambda bi, i: (0, 0))
    return pl.pallas_call(
        _swa_kernel,
        grid=(b, t // blk),
        in_specs=[cur(B_W), cur(2 * B_KVW), prev(2 * B_KVW), cur(1), prev(1), cur(B_W),
                  full(B_W), full(LANES), full(LANES)],
        out_specs=cur(B_W),
        out_shape=jax.ShapeDtypeStruct((b, t, B_W), F32),
        compiler_params=_cparams(2),
        name="swa",
    )(q, kv, kv, pos3, pos3, gate, sink_lanes, freq, sgn)


def _out_proj_kernel(x_ref, ya_ref, yb_ref, w_ref, o_ref):
    y = jnp.concatenate([ya_ref[...].astype(BF16), yb_ref[...].astype(BF16)], axis=1)
    o_ref[...] = x_ref[...] + jnp.dot(y, w_ref[...], preferred_element_type=F32)


def _out_proj(x2d, ya, yb, w_bf16, tm):
    n, d = x2d.shape
    return pl.pallas_call(
        _out_proj_kernel,
        grid=(n // tm,),
        in_specs=[
            pl.BlockSpec((tm, d), lambda i: (i, 0)),
            pl.BlockSpec((tm, A_W), lambda i: (i, 0)),
            pl.BlockSpec((tm, B_W), lambda i: (i, 0)),
            pl.BlockSpec((A_W + B_W, d), lambda i: (0, 0), pipeline_mode=pl.Buffered(1)),
        ],
        out_specs=pl.BlockSpec((tm, d), lambda i: (i, 0)),
        out_shape=jax.ShapeDtypeStruct((n, d), F32),
        compiler_params=_cparams(1),
        name="out_proj",
    )(x2d, ya, yb, w_bf16)


def _sgu_out_kernel(x_ref, u_ref, v_ref, gate_ref, lnw_ref, lnb_ref, ws_ref, bst_ref, w_ref,
                    fin_ref, o_ref, y_ref, *, final_norm):
    tm = x_ref.shape[0]
    ch = SGU_CHUNK
    v = v_ref[...]
    mean = jnp.mean(v, axis=-1, keepdims=True)
    d = v - mean
    var = jnp.mean(d * d, axis=-1, keepdims=True)
    vn = (d * lax.rsqrt(var + LN_EPS)) * lnw_ref[...] + lnb_ref[...]
    ti = lax.broadcasted_iota(jnp.int32, (ch, ch), 0)
    si = lax.broadcasted_iota(jnp.int32, (ch, ch), 1)
    causal = ti >= si
    for g in range(SGU_GROUPS):
        sl = slice(g * LANES, (g + 1) * LANES)
        wm = jnp.where(causal, ws_ref[g], 0.0).astype(BF16)
        bias = bst_ref[:, g:g + 1]
        for ci in range(tm // ch):
            rows = slice(ci * ch, (ci + 1) * ch)
            mixed = jnp.dot(wm, vn[rows, sl].astype(BF16), preferred_element_type=F32) + bias
            y = u_ref[rows, sl] * mixed * _silu(gate_ref[rows, sl])
            y_ref[rows, sl] = y.astype(BF16)
    out = x_ref[...] + jnp.dot(y_ref[...], w_ref[...], preferred_element_type=F32)
    if final_norm:
        ms = jnp.mean(out * out, axis=-1, keepdims=True)
        out = (out * lax.rsqrt(ms + RMS_EPS)) * fin_ref[...]
    o_ref[...] = out


def _sgu_out(x2d, u, v, gate, ln_w, ln_b, ws, bs, w_bf16, final_g, tm):
    n, d = x2d.shape
    final_norm = final_g is not None
    fin = final_g if final_norm else jnp.ones((d,), F32)
    row_tile = pl.BlockSpec((tm, d), lambda i: (i, 0))
    vec = pl.BlockSpec((1, d), lambda i: (0, 0))
    return pl.pallas_call(
        functools.partial(_sgu_out_kernel, final_norm=final_norm),
        grid=(n // tm,),
        in_specs=[
            row_tile, row_tile, row_tile, row_tile, vec, vec,
            pl.BlockSpec((SGU_GROUPS, SGU_CHUNK, SGU_CHUNK), lambda i: (0, 0, 0)),
            pl.BlockSpec((SGU_CHUNK, SGU_GROUPS), lambda i: (0, 0)),
            pl.BlockSpec((d, d), lambda i: (0, 0), pipeline_mode=pl.Buffered(1)),
            vec,
        ],
        out_specs=row_tile,
        out_shape=jax.ShapeDtypeStruct((n, d), F32),
        scratch_shapes=[pltpu.VMEM((tm, d), BF16)],
        compiler_params=_cparams(1),
        name="sgu_out",
    )(x2d, u, v, gate, ln_w.reshape(1, d), ln_b.reshape(1, d), ws, bs.T, w_bf16, fin.reshape(1, d))


def kernel(x, positions, e_norm, e_w_in, e_mu, rwkv_w0, rwkv_w2, rwkv_a0, rwkv_a2, rwkv_k_k, rwkv_k_a, rwkv_r_k, rwkv_ln_w, rwkv_ln_b, rwkv_v0, rwkv_v1, rwkv_v2, attn_sinks, e_w_out, o_norm, o_w_in, sgu_ln_w, sgu_ln_b, sgu_ws, sgu_bs, o_w_out, final_norm):
    b, t, d = x.shape
    n = b * t
    depth = e_norm.shape[0] + o_norm.shape[0]
    assert t % 256 == 0 and d == D_MODEL and depth % 2 == 0
    x2d = x.reshape(n, d)
    v_first = None
    for layer in range(depth):
        if layer % 2 == 0:
            e = layer // 2
            a_in, a_gate, q, kv, b_gate = _norm_matmul(
                x2d, e_norm[e], e_w_in[e].astype(BF16),
                (SHIFT_W, A_W, B_W, 2 * B_KVW, B_W), tm=256)
            r3 = lambda z: z.reshape(b, t, z.shape[-1])
            vres = None if e == 0 else (v_first, rwkv_v0[e - 1], rwkv_v1[e - 1], rwkv_v2[e - 1])
            r, k2, v, na, nb, lw, cs = _rwkv_prep(
                r3(a_in), e_mu[e], rwkv_w0[e], rwkv_w2[e], rwkv_a0[e], rwkv_a2[e],
                rwkv_k_k[e], rwkv_k_a[e], vres, tt=256)
            if e == 0:
                v_first = v
            ya = _wkv(r, k2, v, na, nb, lw, cs, r3(a_gate), rwkv_ln_w[e], rwkv_ln_b[e], rwkv_r_k[e])
            yb = _swa(r3(q), r3(kv), positions, r3(b_gate), attn_sinks[e])
            x2d = _out_proj(x2d, ya.reshape(n, A_W), yb.reshape(n, B_W), e_w_out[e].astype(BF16), tm=512)
        else:
            o = layer // 2
            u, vv, gate = _norm_matmul(x2d, o_norm[o], o_w_in[o].astype(BF16), (d, d, d), tm=256)
            fin = final_norm if layer == depth - 1 else None
            x2d = _sgu_out(x2d, u, vv, gate, sgu_ln_w[o], sgu_ln_b[o], sgu_ws[o], sgu_bs[o],
                           o_w_out[o].astype(BF16), fin, tm=256)
    return x2d.reshape(b, t, d)
```

```python
import functools
import math

import jax
import jax.numpy as jnp
from jax import lax
from jax.experimental import pallas as pl
from jax.experimental.pallas import tpu as pltpu

F32 = jnp.float32
BF16 = jnp.bfloat16

D_MODEL = 2048
HEAD = 64
A_W = 1024
B_W = 1024
B_KVW = 256
B_GROUP = 4
LORA = 64
SHIFT_W = 3 * A_W + 2 * LORA
WINDOW = 128
ROPE_DIM = 16
ROPE_THETA = 500000.0
SGU_CHUNK = 128
SGU_GROUPS = 16
RMS_EPS = 1e-5
LN_EPS = 1e-5
LN_X_EPS = HEAD * 1e-5

LANES = 128
WKV_CHUNK = 64
VMEM_LIMIT = 56 * 1024 * 1024


def _cparams(n_axes):
    return pltpu.CompilerParams(
        dimension_semantics=("arbitrary",) * n_axes, vmem_limit_bytes=VMEM_LIMIT)


def _bdot(a, b):
    return jnp.dot(a.astype(BF16), b.astype(BF16), preferred_element_type=F32)


def _bdot_nt(a, b):
    return lax.dot_general(a.astype(BF16), b.astype(BF16), (((1,), (1,)), ((), ())),
                           preferred_element_type=F32)


def _bdot_tn(a, b):
    return lax.dot_general(a.astype(BF16), b.astype(BF16), (((0,), (0,)), ((), ())),
                           preferred_element_type=F32)


def _split_dot(x, ones_bf16, passes):
    acc = None
    rem = x
    for _ in range(passes):
        piece = rem.astype(BF16)
        term = jnp.dot(piece, ones_bf16, preferred_element_type=F32)
        acc = term if acc is None else acc + term
        rem = rem - piece.astype(F32)
    return acc


def _silu(x):
    return x * (1.0 / (1.0 + jnp.exp(-x)))


def _sigmoid(x):
    return 1.0 / (1.0 + jnp.exp(-x))


def _head_masks():
    lane = lax.broadcasted_iota(jnp.int32, (1, LANES), 1)
    return lane < HEAD, lane >= HEAD


def _pair_blockdiag(x, m0, m1):
    zero = jnp.zeros_like(x)
    return jnp.concatenate([jnp.where(m0, x, zero), jnp.where(m1, x, zero)], axis=0)


def _norm_matmul_kernel(x_ref, g_ref, w_ref, *out_refs, seg_widths, col_chunk):
    x = x_ref[...]
    ms = jnp.mean(x * x, axis=-1, keepdims=True)
    h = ((x * lax.rsqrt(ms + RMS_EPS)) * g_ref[...]).astype(BF16)
    col = 0
    for o_ref, width in zip(out_refs, seg_widths):
        for c0 in range(0, width, col_chunk):
            cw = min(col_chunk, width - c0)
            o_ref[:, c0:c0 + cw] = jnp.dot(
                h, w_ref[:, col + c0:col + c0 + cw], preferred_element_type=F32)
        col += width


def _norm_matmul(x2d, g, w_bf16, seg_widths, tm):
    n, d = x2d.shape
    ncol = w_bf16.shape[1]
    assert sum(seg_widths) == ncol and n % tm == 0
    kern = functools.partial(_norm_matmul_kernel, seg_widths=tuple(seg_widths), col_chunk=512)
    return pl.pallas_call(
        kern,
        grid=(n // tm,),
        in_specs=[
            pl.BlockSpec((tm, d), lambda i: (i, 0)),
            pl.BlockSpec((1, d), lambda i: (0, 0)),
            pl.BlockSpec((d, ncol), lambda i: (0, 0), pipeline_mode=pl.Buffered(1)),
        ],
        out_specs=[pl.BlockSpec((tm, w), lambda i: (i, 0)) for w in seg_widths],
        out_shape=[jax.ShapeDtypeStruct((n, w), F32) for w in seg_widths],
        compiler_params=_cparams(1),
        name="norm_matmul",
    )(x2d, g.reshape(1, d), w_bf16)


def _rwkv_prep_kernel(*refs, has_vres):
    if has_vres:
        (z_ref, zp_ref, mu_ref, w0_ref, a0_ref, wa2_ref, kk_ref, ka_ref, tri_ref, ones_ref,
         vf_ref, v0_ref, v1_ref, v2_ref,
         r_out, k_out, v_out, na_out, nb_out, lw_out, cs_out) = refs
    else:
        (z_ref, zp_ref, mu_ref, w0_ref, a0_ref, wa2_ref, kk_ref, ka_ref, tri_ref, ones_ref,
         r_out, k_out, v_out, na_out, nb_out, lw_out, cs_out) = refs
    i = pl.program_id(1)
    z = z_ref[0]
    tt = z.shape[0]
    prev_last = jnp.where(i > 0, zp_ref[0, 7:8, :], 0.0)
    rolled = pltpu.roll(z, 1, axis=0)
    row = lax.broadcasted_iota(jnp.int32, (tt, 1), 0)
    prev = jnp.where(row == 0, prev_last, rolled)
    zz = z + (prev - z) * mu_ref[...]
    r = zz[:, 0:A_W]
    k = zz[:, A_W:2 * A_W]
    v = zz[:, 2 * A_W:3 * A_W]
    xwa = zz[:, 3 * A_W:3 * A_W + 2 * LORA]
    lane = lax.broadcasted_iota(jnp.int32, (1, 2 * LORA), 1)
    lora_in = jnp.where(lane < LORA, jnp.tanh(xwa), xwa)
    lora = _bdot(lora_in, wa2_ref[...])
    wpre = w0_ref[...] + lora[:, 0:A_W]
    neg = -wpre
    softplus = jnp.maximum(neg, 0.0) + jnp.log(1.0 + jnp.exp(-jnp.abs(neg)))
    w_log = -softplus - 0.5
    lw = -jnp.exp(w_log)
    a = _sigmoid(a0_ref[...] + lora[:, A_W:2 * A_W])
    if has_vres:
        vf = vf_ref[0]
        gate = _sigmoid(v0_ref[...] + _bdot(_bdot(v, v1_ref[...]), v2_ref[...]))
        v = v + (vf - v) * gate
    kk = k * kk_ref[...]
    k2 = k * (1.0 + (a - 1.0) * ka_ref[...])
    ones_bd = ones_ref[...]
    tri = tri_ref[...]
    for p in range(A_W // LANES):
        sl = slice(p * LANES, (p + 1) * LANES)
        kkp = kk[:, sl]
        ss = _split_dot(kkp * kkp, ones_bd, 2)
        kkn = kkp * lax.rsqrt(ss + 1e-12)
        na_out[0, :, sl] = -kkn
        nb_out[0, :, sl] = kkn * a[:, sl]
    r_out[0] = r
    k_out[0] = k2
    v_out[0] = v
    lw_out[0] = lw
    acc = None
    rem = lw
    for _ in range(3):
        piece = rem.astype(BF16)
        term = jnp.dot(tri, piece, preferred_element_type=F32)
        acc = term if acc is None else acc + term
        rem = rem - piece.astype(F32)
    cs_out[0] = acc


def _rwkv_prep(a_in, mu, w0, w2, a0, a2, k_k, k_a, vres, tt):
    b, t, _ = a_in.shape
    has_vres = vres is not None
    wa2 = jnp.zeros((2 * LORA, 2 * A_W), F32)
    wa2 = wa2.at[:LORA, :A_W].set(w2).at[LORA:, A_W:].set(a2).astype(BF16)
    ti = jnp.arange(tt)
    tri = ((ti[:, None] >= ti[None, :]) &
           (ti[:, None] // WKV_CHUNK == ti[None, :] // WKV_CHUNK)).astype(BF16)
    li = jnp.arange(LANES)
    ones_bd = (li[:, None] // HEAD == li[None, :] // HEAD).astype(BF16)
    row = lambda p: p.reshape(1, -1)
    full = lambda shape: pl.BlockSpec(shape, lambda bi, i: (0,) * len(shape))
    tile = lambda w: pl.BlockSpec((1, tt, w), lambda bi, i: (bi, i, 0))
    in_specs = [
        tile(SHIFT_W),
        pl.BlockSpec((1, 8, SHIFT_W), lambda bi, i: (bi, jnp.maximum(i * (tt // 8) - 1, 0), 0)),
        full((1, SHIFT_W)), full((1, A_W)), full((1, A_W)), full((2 * LORA, 2 * A_W)),
        full((1, A_W)), full((1, A_W)), full((tt, tt)), full((LANES, LANES)),
    ]
    args = [a_in, a_in, row(mu), row(w0), row(a0), wa2, row(k_k), row(k_a), tri, ones_bd]
    if has_vres:
        v_first, v0, v1, v2 = vres
        in_specs += [tile(A_W), full((1, A_W)), full(v1.shape), full(v2.shape)]
        args += [v_first, row(v0), v1.astype(BF16), v2.astype(BF16)]
    return pl.pallas_call(
        functools.partial(_rwkv_prep_kernel, has_vres=has_vres),
        grid=(b, t // tt),
        in_specs=in_specs,
        out_specs=[tile(A_W)] * 7,
        out_shape=[jax.ShapeDtypeStruct((b, t, A_W), F32)] * 7,
        compiler_params=_cparams(2),
        name="rwkv_prep",
    )(*args)


def _wkv_kernel(r_ref, k_ref, v_ref, na_ref, nb_ref, lw_ref, cs_ref, g_ref,
                lnw_ref, lnb_ref, rk_ref, ones_ref, o_ref, s_ref):
    c = WKV_CHUNK

    @pl.when(pl.program_id(1) == 0)
    def _():
        s_ref[...] = jnp.zeros_like(s_ref)

    m0, m1 = _head_masks()
    ti = lax.broadcasted_iota(jnp.int32, (c, LANES), 0)
    tj = lax.broadcasted_iota(jnp.int32, (c, LANES), 1) & (c - 1)
    strict = ti > tj
    incl = ti >= tj
    eye_t = jnp.where(ti == tj, 1.0, 0.0)
    li = lax.broadcasted_iota(jnp.int32, (LANES, LANES), 0)
    lj = lax.broadcasted_iota(jnp.int32, (LANES, LANES), 1)
    bdmask = (li >= HEAD) == (lj >= HEAD)
    eye_full = li == lj
    ones_bd = ones_ref[...]
    bd = lambda x: _pair_blockdiag(x, m0, m1)

    pairs = range(A_W // LANES)
    sls = [slice(p * LANES, (p + 1) * LANES) for p in pairs]
    rt, at_abs, bh, kh, wc, em, lhs0, rhs0 = [], [], [], [], [], [], [], []
    for sl in sls:
        cs = cs_ref[0, :, sl]
        mid = cs[c // 2 - 1:c // 2, :]
        last = cs[c - 1:c, :]
        e_neg = jnp.exp(mid - cs)
        rt_p = r_ref[0, :, sl] * jnp.exp(cs - mid)
        at_p = na_ref[0, :, sl] * jnp.exp(cs - lw_ref[0, :, sl] - mid)
        kt_p = k_ref[0, :, sl] * e_neg
        bt_p = nb_ref[0, :, sl] * e_neg
        em_p = jnp.exp(mid)
        e_end = jnp.exp(last - mid)
        rt.append(rt_p)
        at_abs.append(at_p * em_p)
        em.append(em_p)
        wc.append(jnp.exp(last))
        bh.append(bt_p * e_end)
        kh.append(kt_p * e_end)
        lhs0.append(jnp.concatenate([at_p, rt_p], axis=0))
        rhs0.append(jnp.concatenate([bd(bt_p), bd(kt_p)], axis=0))
    sc = [_bdot_nt(lhs0[p], rhs0[p]) for p in pairs]
    a_ab = [jnp.where(strict, sc[p][:c, :LANES], 0.0) for p in pairs]
    a_rb = [jnp.where(incl, sc[p][c:, :LANES], 0.0) for p in pairs]
    akrk = [jnp.concatenate([jnp.where(strict, sc[p][:c, LANES:], 0.0),
                             jnp.where(incl, sc[p][c:, LANES:], 0.0)], axis=0) for p in pairs]
    xv = [_bdot(akrk[p], bd(v_ref[0, :, sls[p]])) for p in pairs]
    apow = a_ab
    pinv = [eye_t for _ in pairs]
    for _ in range(5):
        both = [_bdot(jnp.concatenate([apow[p], pinv[p]], axis=0), bd(apow[p])) for p in pairs]
        pinv = [pinv[p] + both[p][c:] for p in pairs]
        apow = [both[p][:c] for p in pairs]
    pinv = [pinv[p] + _bdot(pinv[p], bd(apow[p])) for p in pairs]
    ta = [_bdot(pinv[p], jnp.concatenate([bd(xv[p][:c]), bd(at_abs[p])], axis=1)) for p in pairs]
    rb = [_bdot(a_rb[p], jnp.concatenate([bd(ta[p][:, LANES:]), bd(ta[p][:, :LANES])], axis=1))
          for p in pairs]
    gram = [_bdot_tn(jnp.concatenate([ta[p][:, LANES:], ta[p][:, :LANES], v_ref[0, :, sls[p]]], axis=1),
                     jnp.concatenate([bh[p], kh[p]], axis=1)) for p in pairs]
    y = []
    for p in pairs:
        rp = rt[p] * em[p] + rb[p][:, :LANES]
        yv = rb[p][:, LANES:] + xv[p][c:]
        y.append(_bdot_nt(rp, s_ref[p]) + yv)
    for p in pairs:
        g = gram[p]
        m_mat = jnp.where(eye_full, wc[p], 0.0) + jnp.where(bdmask, g[:LANES, :LANES], 0.0)
        n_mat = jnp.where(bdmask, g[LANES:2 * LANES, :LANES] + g[2 * LANES:, LANES:], 0.0)
        s_ref[p] = _bdot(s_ref[p], m_mat) + n_mat
    stats = [_split_dot(jnp.concatenate(
        [y[p], r_ref[0, :, sls[p]] * k_ref[0, :, sls[p]] * rk_ref[:, sls[p]]], axis=0), ones_bd, 2)
        for p in pairs]
    d = [y[p] - stats[p][:c] * (1.0 / HEAD) for p in pairs]
    var = [_split_dot(d[p] * d[p], ones_bd, 2) * (1.0 / HEAD) for p in pairs]
    for p in pairs:
        sl = sls[p]
        yn = d[p] * lax.rsqrt(var[p] + LN_X_EPS) * lnw_ref[:, sl] + lnb_ref[:, sl]
        out = yn + stats[p][c:] * v_ref[0, :, sl]
        o_ref[0, :, sl] = out * _silu(g_ref[0, :, sl])


def _wkv(r, k, v, na, nb, lw, cs, gate, ln_w, ln_b, r_k):
    b, t, _ = r.shape
    c = WKV_CHUNK
    li = jnp.arange(LANES)
    ones_bd = (li[:, None] // HEAD == li[None, :] // HEAD).astype(BF16)
    tile = pl.BlockSpec((1, c, A_W), lambda bi, i: (bi, i, 0))
    vec = pl.BlockSpec((1, A_W), lambda bi, i: (0, 0))
    return pl.pallas_call(
        _wkv_kernel,
        grid=(b, t // c),
        in_specs=[tile] * 8 + [vec] * 3 + [pl.BlockSpec((LANES, LANES), lambda bi, i: (0, 0))],
        out_specs=tile,
        out_shape=jax.ShapeDtypeStruct((b, t, A_W), F32),
        scratch_shapes=[pltpu.VMEM((A_W // LANES, LANES, LANES), F32)],
        compiler_params=_cparams(2),
        name="wkv",
    )(r, k, v, na, nb, lw, cs, gate, ln_w.reshape(1, A_W), ln_b.reshape(1, A_W),
      r_k.reshape(1, A_W), ones_bd)


def _rope_tables(pos_col, freq, sgn):
    ang = pos_col.astype(F32) * freq
    return jnp.cos(ang), jnp.sin(ang) * sgn


def _rope(x, cos_t, sin_t, first8):
    fwd = pltpu.roll(x, LANES - ROPE_DIM // 2, axis=1)
    bwd = pltpu.roll(x, ROPE_DIM // 2, axis=1)
    return x * cos_t + jnp.where(first8, fwd, bwd) * sin_t


def _dup_head(tile, which, m0):
    swapped = pltpu.roll(tile, HEAD, axis=1)
    if which == 0:
        return jnp.where(m0, tile, swapped)
    return jnp.where(m0, swapped, tile)


def _swa_kernel(q_ref, kvc_ref, kvp_ref, posc_ref, posp_ref, g_ref, sink_ref, freq_ref, sgn_ref,
                o_ref):
    n = pl.program_id(1)
    blk = WINDOW
    m0, m1 = _head_masks()
    lane = lax.broadcasted_iota(jnp.int32, (1, LANES), 1)
    first8 = (lane & (HEAD - 1)) < ROPE_DIM // 2
    cos_c, sin_c = _rope_tables(posc_ref[0], freq_ref[...], sgn_ref[...])
    cos_p, sin_p = _rope_tables(posp_ref[0], freq_ref[...], sgn_ref[...])
    qi = lax.broadcasted_iota(jnp.int32, (blk, 2 * blk), 0) + blk
    ki = lax.broadcasted_iota(jnp.int32, (blk, 2 * blk), 1)
    valid = (ki <= qi) & (qi - ki < WINDOW) & ((n > 0) | (ki >= blk))
    scale = 1.0 / math.sqrt(HEAD)
    n_kv_tiles = B_KVW // LANES
    for kt in range(n_kv_tiles):
        k_tile = jnp.concatenate(
            [_rope(kvp_ref[0, :, kt * LANES:(kt + 1) * LANES], cos_p, sin_p, first8),
             _rope(kvc_ref[0, :, kt * LANES:(kt + 1) * LANES], cos_c, sin_c, first8)], axis=0)
        v_tile = jnp.concatenate(
            [kvp_ref[0, :, B_KVW + kt * LANES:B_KVW + (kt + 1) * LANES],
             kvc_ref[0, :, B_KVW + kt * LANES:B_KVW + (kt + 1) * LANES]], axis=0)
        for which in range(2):
            g = 2 * kt + which
            k2 = _dup_head(k_tile, which, m0).astype(BF16)
            v2 = _dup_head(v_tile, which, m0)
            v2_h0 = jnp.where(m0, v2, 0.0).astype(BF16)
            v2_h1 = jnp.where(m1, v2, 0.0).astype(BF16)
            for pr in range(B_GROUP // 2):
                qt = 2 * g + pr
                sl = slice(qt * LANES, (qt + 1) * LANES)
                qp = _rope(q_ref[0, :, sl], cos_c, sin_c, first8) * scale
                acc = None
                inv = None
                for half, (msk, vh) in enumerate(((m0, v2_h0), (m1, v2_h1))):
                    qh = jnp.where(msk, qp, 0.0)
                    s = _bdot_nt(qh, k2)
                    s = jnp.where(valid, s, -jnp.inf)
                    sink = sink_ref[:, qt * LANES + half * HEAD:qt * LANES + half * HEAD + 1]
                    m = jnp.maximum(jnp.max(s, axis=-1, keepdims=True), sink)
                    pexp = jnp.exp(s - m)
                    den = jnp.sum(pexp, axis=-1, keepdims=True) + jnp.exp(sink - m)
                    o = jnp.dot(pexp.astype(BF16), vh, preferred_element_type=F32)
                    acc = o if acc is None else acc + o
                    r_den = 1.0 / den
                    inv = r_den if inv is None else jnp.where(m0, inv, r_den)
                o_ref[0, :, sl] = acc * inv * _silu(g_ref[0, :, sl])


def _swa(q, kv, positions, gate, sinks):
    b, t, _ = q.shape
    blk = WINDOW
    half = ROPE_DIM // 2
    li = jnp.arange(LANES) % HEAD
    inv_freq = jnp.power(jnp.float32(ROPE_THETA), -jnp.arange(half, dtype=F32) / half)
    freq = jnp.where(li < ROPE_DIM, inv_freq[li % half], 0.0).reshape(1, LANES).astype(F32)
    sgn = jnp.where(li < half, -1.0, jnp.where(li < ROPE_DIM, 1.0, 0.0)).reshape(1, LANES).astype(F32)
    sink_lanes = jnp.repeat(sinks.astype(F32), HEAD).reshape(1, B_W)
    pos3 = positions.reshape(b, t, 1)
    cur = lambda w: pl.BlockSpec((1, blk, w), lambda bi, i: (bi, i, 0))
    prev = lambda w: pl.BlockSpec((1, blk, w), lambda bi, i: (bi, jnp.maximum(i - 1, 0), 0))
    full = lambda w: pl.BlockSpec((1, w), lambda bi, i: (0, 0))
    return pl.pallas_call(
        _swa_kernel,
        grid=(b, t // blk),
        in_specs=[cur(B_W), cur(2 * B_KVW), prev(2 * B_KVW), cur(1), prev(1), cur(B_W),
                  full(B_W), full(LANES), full(LANES)],
        out_specs=cur(B_W),
        out_shape=jax.ShapeDtypeStruct((b, t, B_W), F32),
        compiler_params=_cparams(2),
        name="swa",
    )(q, kv, kv, pos3, pos3, gate, sink_lanes, freq, sgn)


def _out_proj_kernel(x_ref, ya_ref, yb_ref, w_ref, o_ref):
    y = jnp.concatenate([ya_ref[...].astype(BF16), yb_ref[...].astype(BF16)], axis=1)
    o_ref[...] = x_ref[...] + jnp.dot(y, w_ref[...], preferred_element_type=F32)


def _out_proj(x2d, ya, yb, w_bf16, tm):
    n, d = x2d.shape
    return pl.pallas_call(
        _out_proj_kernel,
        grid=(n // tm,),
        in_specs=[
            pl.BlockSpec((tm, d), lambda i: (i, 0)),
            pl.BlockSpec((tm, A_W), lambda i: (i, 0)),
            pl.BlockSpec((tm, B_W), lambda i: (i, 0)),
            pl.BlockSpec((A_W + B_W, d), lambda i: (0, 0), pipeline_mode=pl.Buffered(1)),
        ],
        out_specs=pl.BlockSpec((tm, d), lambda i: (i, 0)),
        out_shape=jax.ShapeDtypeStruct((n, d), F32),
        compiler_params=_cparams(1),
        name="out_proj",
    )(x2d, ya, yb, w_bf16)


def _sgu_out_kernel(x_ref, u_ref, v_ref, gate_ref, lnw_ref, lnb_ref, ws_ref, bst_ref, w_ref,
                    fin_ref, o_ref, y_ref, *, final_norm):
    tm = x_ref.shape[0]
    ch = SGU_CHUNK
    v = v_ref[...]
    mean = jnp.mean(v, axis=-1, keepdims=True)
    d = v - mean
    var = jnp.mean(d * d, axis=-1, keepdims=True)
    vn = (d * lax.rsqrt(var + LN_EPS)) * lnw_ref[...] + lnb_ref[...]
    ti = lax.broadcasted_iota(jnp.int32, (ch, ch), 0)
    si = lax.broadcasted_iota(jnp.int32, (ch, ch), 1)
    causal = ti >= si
    for g in range(SGU_GROUPS):
        sl = slice(g * LANES, (g + 1) * LANES)
        wm = jnp.where(causal, ws_ref[g], 0.0).astype(BF16)
        bias = bst_ref[:, g:g + 1]
        for ci in range(tm // ch):
            rows = slice(ci * ch, (ci + 1) * ch)
            mixed = jnp.dot(wm, vn[rows, sl].astype(BF16), preferred_element_type=F32) + bias
            y = u_ref[rows, sl] * mixed * _silu(gate_ref[rows, sl])
            y_ref[rows, sl] = y.astype(BF16)
    out = x_ref[...] + jnp.dot(y_ref[...], w_ref[...], preferred_element_type=F32)
    if final_norm:
        ms = jnp.mean(out * out, axis=-1, keepdims=True)
        out = (out * lax.rsqrt(ms + RMS_EPS)) * fin_ref[...]
    o_ref[...] = out


def _sgu_out(x2d, u, v, gate, ln_w, ln_b, ws, bs, w_bf16, final_g, tm):
    n, d = x2d.shape
    final_norm = final_g is not None
    fin = final_g if final_norm else jnp.ones((d,), F32)
    row_tile = pl.BlockSpec((tm, d), lambda i: (i, 0))
    vec = pl.BlockSpec((1, d), lambda i: (0, 0))
    return pl.pallas_call(
        functools.partial(_sgu_out_kernel, final_norm=final_norm),
        grid=(n // tm,),
        in_specs=[
            row_tile, row_tile, row_tile, row_tile, vec, vec,
            pl.BlockSpec((SGU_GROUPS, SGU_CHUNK, SGU_CHUNK), lambda i: (0, 0, 0)),
            pl.BlockSpec((SGU_CHUNK, SGU_GROUPS), lambda i: (0, 0)),
            pl.BlockSpec((d, d), lambda i: (0, 0), pipeline_mode=pl.Buffered(1)),
            vec,
        ],
        out_specs=row_tile,
        out_shape=jax.ShapeDtypeStruct((n, d), F32),
        scratch_shapes=[pltpu.VMEM((tm, d), BF16)],
        compiler_params=_cparams(1),
        name="sgu_out",
    )(x2d, u, v, gate, ln_w.reshape(1, d), ln_b.reshape(1, d), ws, bs.T, w_bf16, fin.reshape(1, d))


def kernel(x, positions, e_norm, e_w_in, e_mu, rwkv_w0, rwkv_w2, rwkv_a0, rwkv_a2, rwkv_k_k, rwkv_k_a, rwkv_r_k, rwkv_ln_w, rwkv_ln_b, rwkv_v0, rwkv_v1, rwkv_v2, attn_sinks, e_w_out, o_norm, o_w_in, sgu_ln_w, sgu_ln_b, sgu_ws, sgu_bs, o_w_out, final_norm):
    b, t, d = x.shape
    n = b * t
    depth = e_norm.shape[0] + o_norm.shape[0]
    assert t % 256 == 0 and d == D_MODEL and depth % 2 == 0
    x2d = x.reshape(n, d)
    v_first = None
    for layer in range(depth):
        if layer % 2 == 0:
            e = layer // 2
            a_in, a_gate, q, kv, b_gate = _norm_matmul(
                x2d, e_norm[e], e_w_in[e].astype(BF16),
                (SHIFT_W, A_W, B_W, 2 * B_KVW, B_W), tm=256)
            r3 = lambda z: z.reshape(b, t, z.shape[-1])
            vres = None if e == 0 else (v_first, rwkv_v0[e - 1], rwkv_v1[e - 1], rwkv_v2[e - 1])
            r, k2, v, na, nb, lw, cs = _rwkv_prep(
                r3(a_in), e_mu[e], rwkv_w0[e], rwkv_w2[e], rwkv_a0[e], rwkv_a2[e],
                rwkv_k_k[e], rwkv_k_a[e], vres, tt=256)
            if e == 0:
                v_first = v
            ya = _wkv(r, k2, v, na, nb, lw, cs, r3(a_gate), rwkv_ln_w[e], rwkv_ln_b[e], rwkv_r_k[e])
            yb = _swa(r3(q), r3(kv), positions, r3(b_gate), attn_sinks[e])
            x2d = _out_proj(x2d, ya.reshape(n, A_W), yb.reshape(n, B_W), e_w_out[e].astype(BF16), tm=512)
        else:
            o = layer // 2
            u, vv, gate = _norm_matmul(x2d, o_norm[o], o_w_in[o].astype(BF16), (d, d, d), tm=256)
            fin = final_norm if layer == depth - 1 else None
            x2d = _sgu_out(x2d, u, vv, gate, sgu_ln_w[o], sgu_ln_b[o], sgu_ws[o], sgu_bs[o],
                           o_w_out[o].astype(BF16), fin, tm=256)
    return x2d.reshape(b, t, d)
```

```python
import functools
import math

import jax
import jax.numpy as jnp
from jax import lax
from jax.experimental import pallas as pl
from jax.experimental.pallas import tpu as pltpu

F32 = jnp.float32
BF16 = jnp.bfloat16

D_MODEL = 2048
HEAD = 64
A_W = 1024
B_W = 1024
B_KVW = 256
B_GROUP = 4
LORA = 64
SHIFT_W = 3 * A_W + 2 * LORA
WINDOW = 128
ROPE_DIM = 16
ROPE_THETA = 500000.0
SGU_CHUNK = 128
SGU_GROUPS = 16
RMS_EPS = 1e-5
LN_EPS = 1e-5
LN_X_EPS = HEAD * 1e-5

LANES = 128
WKV_CHUNK = 64
WKV_GROUP = 4
VMEM_LIMIT = 56 * 1024 * 1024


def _cparams(n_axes):
    return pltpu.CompilerParams(
        dimension_semantics=("arbitrary",) * n_axes, vmem_limit_bytes=VMEM_LIMIT)


def _bdot(a, b):
    return jnp.dot(a.astype(BF16), b.astype(BF16), preferred_element_type=F32)


def _bdot_nt(a, b):
    return lax.dot_general(a.astype(BF16), b.astype(BF16), (((1,), (1,)), ((), ())),
                           preferred_element_type=F32)


def _bdot_tn(a, b):
    return lax.dot_general(a.astype(BF16), b.astype(BF16), (((0,), (0,)), ((), ())),
                           preferred_element_type=F32)


def _split_dot(x, ones_bf16, passes):
    acc = None
    rem = x
    for _ in range(passes):
        piece = rem.astype(BF16)
        term = jnp.dot(piece, ones_bf16, preferred_element_type=F32)
        acc = term if acc is None else acc + term
        rem = rem - piece.astype(F32)
    return acc


def _silu(x):
    return x * (1.0 / (1.0 + jnp.exp(-x)))


def _sigmoid(x):
    return 1.0 / (1.0 + jnp.exp(-x))


def _head_masks():
    lane = lax.broadcasted_iota(jnp.int32, (1, LANES), 1)
    return lane < HEAD, lane >= HEAD


def _norm_matmul_kernel(x_ref, g_ref, w_ref, *out_refs, seg_widths, col_chunk):
    x = x_ref[...]
    ms = jnp.mean(x * x, axis=-1, keepdims=True)
    h = ((x * lax.rsqrt(ms + RMS_EPS)) * g_ref[...]).astype(BF16)
    col = 0
    for o_ref, width in zip(out_refs, seg_widths):
        for c0 in range(0, width, col_chunk):
            cw = min(col_chunk, width - c0)
            o_ref[:, c0:c0 + cw] = jnp.dot(
                h, w_ref[:, col + c0:col + c0 + cw], preferred_element_type=F32)
        col += width


def _norm_matmul(x2d, g, w_bf16, seg_widths, tm):
    n, d = x2d.shape
    ncol = w_bf16.shape[1]
    assert sum(seg_widths) == ncol and n % tm == 0
    kern = functools.partial(_norm_matmul_kernel, seg_widths=tuple(seg_widths), col_chunk=512)
    return pl.pallas_call(
        kern,
        grid=(n // tm,),
        in_specs=[
            pl.BlockSpec((tm, d), lambda i: (i, 0)),
            pl.BlockSpec((1, d), lambda i: (0, 0)),
            pl.BlockSpec((d, ncol), lambda i: (0, 0), pipeline_mode=pl.Buffered(1)),
        ],
        out_specs=[pl.BlockSpec((tm, w), lambda i: (i, 0)) for w in seg_widths],
        out_shape=[jax.ShapeDtypeStruct((n, w), F32) for w in seg_widths],
        compiler_params=_cparams(1),
        name="norm_matmul",
    )(x2d, g.reshape(1, d), w_bf16)


def _rwkv_prep_kernel(*refs, has_vres):
    if has_vres:
        (z_ref, zp_ref, mu_ref, w0_ref, a0_ref, wa2_ref, kk_ref, ka_ref, tri_ref, ones_ref,
         vf_ref, v0_ref, v1_ref, v2_ref,
         r_out, k_out, v_out, na_out, nb_out, lw_out, cs_out) = refs
    else:
        (z_ref, zp_ref, mu_ref, w0_ref, a0_ref, wa2_ref, kk_ref, ka_ref, tri_ref, ones_ref,
         r_out, k_out, v_out, na_out, nb_out, lw_out, cs_out) = refs
    i = pl.program_id(1)
    z = z_ref[0]
    tt = z.shape[0]
    prev_last = jnp.where(i > 0, zp_ref[0, 7:8, :], 0.0)
    rolled = pltpu.roll(z, 1, axis=0)
    row = lax.broadcasted_iota(jnp.int32, (tt, 1), 0)
    prev = jnp.where(row == 0, prev_last, rolled)
    zz = z + (prev - z) * mu_ref[...]
    r = zz[:, 0:A_W]
    k = zz[:, A_W:2 * A_W]
    v = zz[:, 2 * A_W:3 * A_W]
    xwa = zz[:, 3 * A_W:3 * A_W + 2 * LORA]
    lane = lax.broadcasted_iota(jnp.int32, (1, 2 * LORA), 1)
    lora_in = jnp.where(lane < LORA, jnp.tanh(xwa), xwa)
    lora = _bdot(lora_in, wa2_ref[...])
    wpre = w0_ref[...] + lora[:, 0:A_W]
    neg = -wpre
    softplus = jnp.maximum(neg, 0.0) + jnp.log(1.0 + jnp.exp(-jnp.abs(neg)))
    w_log = -softplus - 0.5
    lw = -jnp.exp(w_log)
    a = _sigmoid(a0_ref[...] + lora[:, A_W:2 * A_W])
    if has_vres:
        vf = vf_ref[0]
        gate = _sigmoid(v0_ref[...] + _bdot(_bdot(v, v1_ref[...]), v2_ref[...]))
        v = v + (vf - v) * gate
    kk = k * kk_ref[...]
    k2 = k * (1.0 + (a - 1.0) * ka_ref[...])
    ones_bd = ones_ref[...]
    tri = tri_ref[...]
    for p in range(A_W // LANES):
        sl = slice(p * LANES, (p + 1) * LANES)
        kkp = kk[:, sl]
        ss = _split_dot(kkp * kkp, ones_bd, 2)
        kkn = kkp * lax.rsqrt(ss + 1e-12)
        na_out[0, :, sl] = -kkn
        nb_out[0, :, sl] = kkn * a[:, sl]
    r_out[0] = r
    k_out[0] = k2
    v_out[0] = v
    lw_out[0] = lw
    acc = None
    rem = lw
    for _ in range(3):
        piece = rem.astype(BF16)
        term = jnp.dot(tri, piece, preferred_element_type=F32)
        acc = term if acc is None else acc + term
        rem = rem - piece.astype(F32)
    cs_out[0] = acc


def _rwkv_prep(a_in, mu, w0, w2, a0, a2, k_k, k_a, vres, tt):
    b, t, _ = a_in.shape
    has_vres = vres is not None
    wa2 = jnp.zeros((2 * LORA, 2 * A_W), F32)
    wa2 = wa2.at[:LORA, :A_W].set(w2).at[LORA:, A_W:].set(a2).astype(BF16)
    ti = jnp.arange(tt)
    tri = ((ti[:, None] >= ti[None, :]) &
           (ti[:, None] // WKV_CHUNK == ti[None, :] // WKV_CHUNK)).astype(BF16)
    li = jnp.arange(LANES)
    ones_bd = (li[:, None] // HEAD == li[None, :] // HEAD).astype(BF16)
    row = lambda p: p.reshape(1, -1)
    full = lambda shape: pl.BlockSpec(shape, lambda bi, i: (0,) * len(shape))
    tile = lambda w: pl.BlockSpec((1, tt, w), lambda bi, i: (bi, i, 0))
    in_specs = [
        tile(SHIFT_W),
        pl.BlockSpec((1, 8, SHIFT_W), lambda bi, i: (bi, jnp.maximum(i * (tt // 8) - 1, 0), 0)),
        full((1, SHIFT_W)), full((1, A_W)), full((1, A_W)), full((2 * LORA, 2 * A_W)),
        full((1, A_W)), full((1, A_W)), full((tt, tt)), full((LANES, LANES)),
    ]
    args = [a_in, a_in, row(mu), row(w0), row(a0), wa2, row(k_k), row(k_a), tri, ones_bd]
    if has_vres:
        v_first, v0, v1, v2 = vres
        in_specs += [tile(A_W), full((1, A_W)), full(v1.shape), full(v2.shape)]
        args += [v_first, row(v0), v1.astype(BF16), v2.astype(BF16)]
    return pl.pallas_call(
        functools.partial(_rwkv_prep_kernel, has_vres=has_vres),
        grid=(b, t // tt),
        in_specs=in_specs,
        out_specs=[tile(A_W)] * 7,
        out_shape=[jax.ShapeDtypeStruct((b, t, A_W), F32)] * 7,
        compiler_params=_cparams(2),
        name="rwkv_prep",
    )(*args)


def _wkv_kernel(r_ref, k_ref, v_ref, na_ref, nb_ref, lw_ref, cs_ref, g_ref,
                lnw_ref, lnb_ref, rk_ref, ones_ref, o_ref, s_ref):
    c = WKV_CHUNK

    @pl.when(pl.program_id(0) == 0)
    def _():
        s_ref[...] = jnp.zeros_like(s_ref)

    gw = WKV_GROUP * HEAD
    lane = lax.broadcasted_iota(jnp.int32, (1, gw), 1)
    head_masks = [(lane >= h * HEAD) & (lane < (h + 1) * HEAD) for h in range(WKV_GROUP)]
    ti = lax.broadcasted_iota(jnp.int32, (c, gw), 0)
    tj = lax.broadcasted_iota(jnp.int32, (c, gw), 1) & (c - 1)
    strict = ti > tj
    incl = ti >= tj
    eye_t = jnp.where(ti == tj, 1.0, 0.0)
    li = lax.broadcasted_iota(jnp.int32, (gw, gw), 0)
    lj = lax.broadcasted_iota(jnp.int32, (gw, gw), 1)
    bdmask = (li // HEAD) == (lj // HEAD)
    eye_full = li == lj
    ones_bd = ones_ref[...]

    def bd(x):
        xb = x.astype(BF16)
        zero = jnp.zeros_like(xb)
        return jnp.concatenate([jnp.where(m, xb, zero) for m in head_masks], axis=0)

    chains = [(bi, slice(p * gw, (p + 1) * gw))
              for bi in range(r_ref.shape[0]) for p in range(A_W // gw)]
    pairs = range(len(chains))
    rd = lambda ref, p: ref[chains[p][0], :, chains[p][1]]
    rt, at_abs, bh, kh, wc, em, lhs0, rhs0 = [], [], [], [], [], [], [], []
    for p in pairs:
        cs = rd(cs_ref, p)
        mid = cs[c // 2 - 1:c // 2, :]
        last = cs[c - 1:c, :]
        e_neg = jnp.exp(mid - cs)
        rt_p = rd(r_ref, p) * jnp.exp(cs - mid)
        at_p = rd(na_ref, p) * jnp.exp(cs - rd(lw_ref, p) - mid)
        kt_p = rd(k_ref, p) * e_neg
        bt_p = rd(nb_ref, p) * e_neg
        em_p = jnp.exp(mid)
        e_end = jnp.exp(last - mid)
        rt.append(rt_p)
        at_abs.append(at_p * em_p)
        em.append(em_p)
        wc.append(jnp.exp(last))
        bh.append(bt_p * e_end)
        kh.append(kt_p * e_end)
        lhs0.append(jnp.concatenate([at_p, rt_p], axis=0))
        rhs0.append(jnp.concatenate([bd(bt_p), bd(kt_p)], axis=0))
    sc = [_bdot_nt(lhs0[p], rhs0[p]) for p in pairs]
    a_ab = [jnp.where(strict, sc[p][:c, :gw], 0.0) for p in pairs]
    a_rb = [jnp.where(incl, sc[p][c:, :gw], 0.0) for p in pairs]
    akrk = [jnp.concatenate([jnp.where(strict, sc[p][:c, gw:], 0.0),
                             jnp.where(incl, sc[p][c:, gw:], 0.0)], axis=0) for p in pairs]
    xv = [_bdot(akrk[p], bd(rd(v_ref, p))) for p in pairs]
    pinv = [eye_t + a_ab[p] for p in pairs]
    apow = [_bdot(a_ab[p], bd(a_ab[p])) for p in pairs]
    for _ in range(4):
        both = [_bdot(jnp.concatenate([apow[p], pinv[p]], axis=0), bd(apow[p])) for p in pairs]
        pinv = [pinv[p] + both[p][c:] for p in pairs]
        apow = [both[p][:c] for p in pairs]
    pinv = [pinv[p] + _bdot(pinv[p], bd(apow[p])) for p in pairs]
    ta = [_bdot(pinv[p], jnp.concatenate([bd(xv[p][:c]), bd(at_abs[p])], axis=1)) for p in pairs]
    uv = [ta[p][:, :gw] for p in pairs]
    ap = [ta[p][:, gw:] for p in pairs]
    rb = [_bdot(a_rb[p], jnp.concatenate([bd(ap[p]), bd(uv[p])], axis=1)) for p in pairs]
    m_gram = [_bdot_tn(ap[p], bh[p]) for p in pairs]
    n_gram = [_bdot_tn(jnp.concatenate([uv[p], rd(v_ref, p)], axis=0),
                       jnp.concatenate([bh[p], kh[p]], axis=0)) for p in pairs]
    y = []
    for p in pairs:
        rp = rt[p] * em[p] + rb[p][:, :gw]
        yv = rb[p][:, gw:] + xv[p][c:]
        y.append(_bdot_nt(rp, s_ref[p]) + yv)
    for p in pairs:
        m_mat = jnp.where(eye_full, wc[p], 0.0) + jnp.where(bdmask, m_gram[p], 0.0)
        s_ref[p] = _bdot(s_ref[p], m_mat) + jnp.where(bdmask, n_gram[p], 0.0)
    stats = [_bdot(jnp.concatenate(
        [y[p], rd(r_ref, p) * rd(k_ref, p) * rk_ref[:, chains[p][1]]], axis=0), ones_bd)
        for p in pairs]
    d = [y[p] - stats[p][:c] * (1.0 / HEAD) for p in pairs]
    var = [_bdot(d[p] * d[p], ones_bd) * (1.0 / HEAD) for p in pairs]
    for p in pairs:
        bi, sl = chains[p]
        yn = d[p] * lax.rsqrt(var[p] + LN_X_EPS) * lnw_ref[:, sl] + lnb_ref[:, sl]
        out = yn + stats[p][c:] * rd(v_ref, p)
        o_ref[bi, :, sl] = out * _silu(rd(g_ref, p))


def _wkv(r, k, v, na, nb, lw, cs, gate, ln_w, ln_b, r_k):
    b, t, _ = r.shape
    c = WKV_CHUNK
    gw = WKV_GROUP * HEAD
    li = jnp.arange(gw)
    ones_bd = (li[:, None] // HEAD == li[None, :] // HEAD).astype(BF16)
    tile = pl.BlockSpec((b, c, A_W), lambda i: (0, i, 0))
    vec = pl.BlockSpec((1, A_W), lambda i: (0, 0))
    return pl.pallas_call(
        _wkv_kernel,
        grid=(t // c,),
        in_specs=[tile] * 8 + [vec] * 3 + [pl.BlockSpec((gw, gw), lambda i: (0, 0))],
        out_specs=tile,
        out_shape=jax.ShapeDtypeStruct((b, t, A_W), F32),
        scratch_shapes=[pltpu.VMEM((b * (A_W // gw), gw, gw), F32)],
        compiler_params=_cparams(1),
        name="wkv",
    )(r, k, v, na, nb, lw, cs, gate, ln_w.reshape(1, A_W), ln_b.reshape(1, A_W),
      r_k.reshape(1, A_W), ones_bd)


def _rope_tables_kernel(pos_ref, freq_ref, sgn_ref, cos_ref, sin_ref):
    ang = pos_ref[0].astype(F32) * freq_ref[...]
    cos_ref[0] = jnp.cos(ang)
    sin_ref[0] = jnp.sin(ang) * sgn_ref[...]


def _rope_tables(positions, rows):
    b, t = positions.shape
    half = ROPE_DIM // 2
    li = jnp.arange(LANES) % HEAD
    inv_freq = jnp.power(jnp.float32(ROPE_THETA), -jnp.arange(half, dtype=F32) / half)
    freq = jnp.where(li < ROPE_DIM, inv_freq[li % half], 0.0).reshape(1, LANES).astype(F32)
    sgn = jnp.where(li < half, -1.0, jnp.where(li < ROPE_DIM, 1.0, 0.0)).reshape(1, LANES).astype(F32)
    full = pl.BlockSpec((1, LANES), lambda bi, i: (0, 0))
    tab = pl.BlockSpec((1, rows, LANES), lambda bi, i: (bi, i, 0))
    return pl.pallas_call(
        _rope_tables_kernel,
        grid=(b, t // rows),
        in_specs=[pl.BlockSpec((1, rows, 1), lambda bi, i: (bi, i, 0)), full, full],
        out_specs=[tab, tab],
        out_shape=[jax.ShapeDtypeStruct((b, t, LANES), F32)] * 2,
        compiler_params=_cparams(2),
        name="rope_tables",
    )(positions.reshape(b, t, 1), freq, sgn)


def _rope(x, cos_t, sin_t, first8):
    fwd = pltpu.roll(x, LANES - ROPE_DIM // 2, axis=1)
    bwd = pltpu.roll(x, ROPE_DIM // 2, axis=1)
    return x * cos_t + jnp.where(first8, fwd, bwd) * sin_t


def _dup_head(tile, which, m0):
    swapped = pltpu.roll(tile, HEAD, axis=1)
    if which == 0:
        return jnp.where(m0, tile, swapped)
    return jnp.where(m0, swapped, tile)


def _swa_kernel(q_ref, kvc_ref, kvp_ref, cosc_ref, sinc_ref, cosp_ref, sinp_ref, g_ref, sink_ref,
                o_ref):
    n = pl.program_id(1)
    blk = WINDOW
    m0, m1 = _head_masks()
    lane = lax.broadcasted_iota(jnp.int32, (1, LANES), 1)
    first8 = (lane & (HEAD - 1)) < ROPE_DIM // 2
    cos_c, sin_c = cosc_ref[0], sinc_ref[0]
    cos_p, sin_p = cosp_ref[0], sinp_ref[0]
    qi = lax.broadcasted_iota(jnp.int32, (blk, 2 * blk), 0) + blk
    ki = lax.broadcasted_iota(jnp.int32, (blk, 2 * blk), 1)
    valid = (ki <= qi) & (qi - ki < WINDOW) & ((n > 0) | (ki >= blk))
    scale = 1.0 / math.sqrt(HEAD)
    ri = lax.broadcasted_iota(jnp.int32, (4 * blk, LANES), 0)
    ci = lax.broadcasted_iota(jnp.int32, (4 * blk, LANES), 1)
    sum_cols = jnp.where((ri >= 2 * blk) == (ci >= HEAD), 1.0, 0.0).astype(BF16)
    tile = lambda j: slice(j * LANES, (j + 1) * LANES)
    k2s, rhs_pv = [], []
    for kt in range(B_KVW // LANES):
        k_tile = jnp.concatenate(
            [_rope(kvp_ref[0, :, tile(kt)], cos_p, sin_p, first8),
             _rope(kvc_ref[0, :, tile(kt)], cos_c, sin_c, first8)], axis=0)
        v_tile = jnp.concatenate(
            [kvp_ref[0, :, B_KVW + kt * LANES:B_KVW + (kt + 1) * LANES],
             kvc_ref[0, :, B_KVW + kt * LANES:B_KVW + (kt + 1) * LANES]], axis=0)
        for which in range(2):
            k2s.append(_dup_head(k_tile, which, m0).astype(BF16))
            v2 = _dup_head(v_tile, which, m0)
            vstack = jnp.concatenate([jnp.where(m0, v2, 0.0), jnp.where(m1, v2, 0.0)], axis=0)
            rhs_pv.append(jnp.concatenate([vstack.astype(BF16), sum_cols], axis=1))
    scores = []
    for g in range(B_KVW // HEAD):
        rows = []
        for pr in range(B_GROUP // 2):
            qp = _rope(q_ref[0, :, tile(2 * g + pr)], cos_c, sin_c, first8) * scale
            rows += [jnp.where(m0, qp, 0.0), jnp.where(m1, qp, 0.0)]
        scores.append(_bdot_nt(jnp.concatenate(rows, axis=0), k2s[g]))
    for g in range(B_KVW // HEAD):
        for pr in range(B_GROUP // 2):
            qt = 2 * g + pr
            probs, sink_terms = [], []
            for half in range(2):
                r0 = (2 * pr + half) * blk
                s = jnp.where(valid, scores[g][r0:r0 + blk], -jnp.inf)
                sink = sink_ref[:, qt * LANES + half * HEAD:qt * LANES + half * HEAD + 1]
                m = jnp.maximum(jnp.max(s, axis=-1, keepdims=True), sink)
                probs.append(jnp.exp(s - m).astype(BF16))
                sink_terms.append(jnp.exp(sink - m))
            od = jnp.dot(jnp.concatenate(probs, axis=1), rhs_pv[g], preferred_element_type=F32)
            den = od[:, LANES:] + jnp.where(m0, sink_terms[0], sink_terms[1])
            o_ref[0, :, tile(qt)] = od[:, :LANES] * (1.0 / den) * _silu(g_ref[0, :, tile(qt)])


def _swa(q, kv, cos_t, sin_t, gate, sinks):
    b, t, _ = q.shape
    blk = WINDOW
    sink_lanes = jnp.repeat(sinks.astype(F32), HEAD).reshape(1, B_W)
    cur = lambda w: pl.BlockSpec((1, blk, w), lambda bi, i: (bi, i, 0))
    prev = lambda w: pl.BlockSpec((1, blk, w), lambda bi, i: (bi, jnp.maximum(i - 1, 0), 0))
    return pl.pallas_call(
        _swa_kernel,
        grid=(b, t // blk),
        in_specs=[cur(B_W), cur(2 * B_KVW), prev(2 * B_KVW), cur(LANES), cur(LANES), prev(LANES),
                  prev(LANES), cur(B_W), pl.BlockSpec((1, B_W), lambda bi, i: (0, 0))],
        out_specs=cur(B_W),
        out_shape=jax.ShapeDtypeStruct((b, t, B_W), F32),
        compiler_params=_cparams(2),
        name="swa",
    )(q, kv, kv, cos_t, sin_t, cos_t, sin_t, gate, sink_lanes)


def _out_proj_kernel(x_ref, ya_ref, yb_ref, w_ref, o_ref):
    y = jnp.concatenate([ya_ref[...].astype(BF16), yb_ref[...].astype(BF16)], axis=1)
    o_ref[...] = x_ref[...] + jnp.dot(y, w_ref[...], preferred_element_type=F32)


def _out_proj(x2d, ya, yb, w_bf16, tm):
    n, d = x2d.shape
    return pl.pallas_call(
        _out_proj_kernel,
        grid=(n // tm,),
        in_specs=[
            pl.BlockSpec((tm, d), lambda i: (i, 0)),
            pl.BlockSpec((tm, A_W), lambda i: (i, 0)),
            pl.BlockSpec((tm, B_W), lambda i: (i, 0)),
            pl.BlockSpec((A_W + B_W, d), lambda i: (0, 0), pipeline_mode=pl.Buffered(1)),
        ],
        out_specs=pl.BlockSpec((tm, d), lambda i: (i, 0)),
        out_shape=jax.ShapeDtypeStruct((n, d), F32),
        compiler_params=_cparams(1),
        name="out_proj",
    )(x2d, ya, yb, w_bf16)


def _sgu_out_kernel(x_ref, u_ref, v_ref, gate_ref, lnw_ref, lnb_ref, ws_ref, bst_ref, w_ref,
                    fin_ref, o_ref, y_ref, *, final_norm):
    tm = x_ref.shape[0]
    ch = SGU_CHUNK
    v = v_ref[...]
    mean = jnp.mean(v, axis=-1, keepdims=True)
    d = v - mean
    var = jnp.mean(d * d, axis=-1, keepdims=True)
    vn = (d * lax.rsqrt(var + LN_EPS)) * lnw_ref[...] + lnb_ref[...]
    ti = lax.broadcasted_iota(jnp.int32, (ch, ch), 0)
    si = lax.broadcasted_iota(jnp.int32, (ch, ch), 1)
    causal = ti >= si
    for g in range(SGU_GROUPS):
        sl = slice(g * LANES, (g + 1) * LANES)
        wm = jnp.where(causal, ws_ref[g], 0.0).astype(BF16)
        bias = bst_ref[:, g:g + 1]
        for ci in range(tm // ch):
            rows = slice(ci * ch, (ci + 1) * ch)
            mixed = jnp.dot(wm, vn[rows, sl].astype(BF16), preferred_element_type=F32) + bias
            y = u_ref[rows, sl] * mixed * _silu(gate_ref[rows, sl])
            y_ref[rows, sl] = y.astype(BF16)
    out = x_ref[...] + jnp.dot(y_ref[...], w_ref[...], preferred_element_type=F32)
    if final_norm:
        ms = jnp.mean(out * out, axis=-1, keepdims=True)
        out = (out * lax.rsqrt(ms + RMS_EPS)) * fin_ref[...]
    o_ref[...] = out


def _sgu_out(x2d, u, v, gate, ln_w, ln_b, ws, bs, w_bf16, final_g, tm):
    n, d = x2d.shape
    final_norm = final_g is not None
    fin = final_g if final_norm else jnp.ones((d,), F32)
    row_tile = pl.BlockSpec((tm, d), lambda i: (i, 0))
    vec = pl.BlockSpec((1, d), lambda i: (0, 0))
    return pl.pallas_call(
        functools.partial(_sgu_out_kernel, final_norm=final_norm),
        grid=(n // tm,),
        in_specs=[
            row_tile, row_tile, row_tile, row_tile, vec, vec,
            pl.BlockSpec((SGU_GROUPS, SGU_CHUNK, SGU_CHUNK), lambda i: (0, 0, 0)),
            pl.BlockSpec((SGU_CHUNK, SGU_GROUPS), lambda i: (0, 0)),
            pl.BlockSpec((d, d), lambda i: (0, 0), pipeline_mode=pl.Buffered(1)),
            vec,
        ],
        out_specs=row_tile,
        out_shape=jax.ShapeDtypeStruct((n, d), F32),
        scratch_shapes=[pltpu.VMEM((tm, d), BF16)],
        compiler_params=_cparams(1),
        name="sgu_out",
    )(x2d, u, v, gate, ln_w.reshape(1, d), ln_b.reshape(1, d), ws, bs.T, w_bf16, fin.reshape(1, d))


def kernel(x, positions, e_norm, e_w_in, e_mu, rwkv_w0, rwkv_w2, rwkv_a0, rwkv_a2, rwkv_k_k, rwkv_k_a, rwkv_r_k, rwkv_ln_w, rwkv_ln_b, rwkv_v0, rwkv_v1, rwkv_v2, attn_sinks, e_w_out, o_norm, o_w_in, sgu_ln_w, sgu_ln_b, sgu_ws, sgu_bs, o_w_out, final_norm):
    b, t, d = x.shape
    n = b * t
    depth = e_norm.shape[0] + o_norm.shape[0]
    assert t % 256 == 0 and d == D_MODEL and depth % 2 == 0
    x2d = x.reshape(n, d)
    cos_t, sin_t = _rope_tables(positions, rows=256)
    v_first = None
    for layer in range(depth):
        if layer % 2 == 0:
            e = layer // 2
            a_in, a_gate, q, kv, b_gate = _norm_matmul(
                x2d, e_norm[e], e_w_in[e].astype(BF16),
                (SHIFT_W, A_W, B_W, 2 * B_KVW, B_W), tm=256)
            r3 = lambda z: z.reshape(b, t, z.shape[-1])
            vres = None if e == 0 else (v_first, rwkv_v0[e - 1], rwkv_v1[e - 1], rwkv_v2[e - 1])
            r, k2, v, na, nb, lw, cs = _rwkv_prep(
                r3(a_in), e_mu[e], rwkv_w0[e], rwkv_w2[e], rwkv_a0[e], rwkv_a2[e],
                rwkv_k_k[e], rwkv_k_a[e], vres, tt=256)
            if e == 0:
                v_first = v
            ya = _wkv(r, k2, v, na, nb, lw, cs, r3(a_gate), rwkv_ln_w[e], rwkv_ln_b[e], rwkv_r_k[e])
            yb = _swa(r3(q), r3(kv), cos_t, sin_t, r3(b_gate), attn_sinks[e])
            x2d = _out_proj(x2d, ya.reshape(n, A_W), yb.reshape(n, B_W), e_w_out[e].astype(BF16), tm=512)
        else:
            o = layer // 2
            u, vv, gate = _norm_matmul(x2d, o_norm[o], o_w_in[o].astype(BF16), (d, d, d), tm=256)
            fin = final_norm if layer == depth - 1 else None
            x2d = _sgu_out(x2d, u, vv, gate, sgu_ln_w[o], sgu_ln_b[o], sgu_ws[o], sgu_bs[o],
                           o_w_out[o].astype(BF16), fin, tm=256)
    return x2d.reshape(b, t, d)
```

```python
import functools
import math

import jax
import jax.numpy as jnp
from jax import lax
from jax.experimental import pallas as pl
from jax.experimental.pallas import tpu as pltpu

F32 = jnp.float32
BF16 = jnp.bfloat16

D_MODEL = 2048
HEAD = 64
A_W = 1024
B_W = 1024
B_KVW = 256
B_GROUP = 4
LORA = 64
SHIFT_W = 3 * A_W + 2 * LORA
WINDOW = 128
ROPE_DIM = 16
ROPE_THETA = 500000.0
SGU_CHUNK = 128
SGU_GROUPS = 16
RMS_EPS = 1e-5
LN_EPS = 1e-5
LN_X_EPS = HEAD * 1e-5

LANES = 128
WKV_CHUNK = 64
WKV_GROUP = 4
WKV_STEP_CHUNKS = 2
VMEM_LIMIT = 56 * 1024 * 1024


def _cparams(n_axes):
    return pltpu.CompilerParams(
        dimension_semantics=("arbitrary",) * n_axes, vmem_limit_bytes=VMEM_LIMIT)


def _bdot(a, b):
    return jnp.dot(a.astype(BF16), b.astype(BF16), preferred_element_type=F32)


def _bdot_nt(a, b):
    return lax.dot_general(a.astype(BF16), b.astype(BF16), (((1,), (1,)), ((), ())),
                           preferred_element_type=F32)


def _bdot_tn(a, b):
    return lax.dot_general(a.astype(BF16), b.astype(BF16), (((0,), (0,)), ((), ())),
                           preferred_element_type=F32)


def _split_dot(x, ones_bf16, passes):
    acc = None
    rem = x
    for _ in range(passes):
        piece = rem.astype(BF16)
        term = jnp.dot(piece, ones_bf16, preferred_element_type=F32)
        acc = term if acc is None else acc + term
        rem = rem - piece.astype(F32)
    return acc


def _silu(x):
    return x * (1.0 / (1.0 + jnp.exp(-x)))


def _sigmoid(x):
    return 1.0 / (1.0 + jnp.exp(-x))


def _head_masks():
    lane = lax.broadcasted_iota(jnp.int32, (1, LANES), 1)
    return lane < HEAD, lane >= HEAD


def _norm_matmul_kernel(x_ref, g_ref, w_ref, *out_refs, seg_widths, col_chunk):
    x = x_ref[...]
    ms = jnp.mean(x * x, axis=-1, keepdims=True)
    h = ((x * lax.rsqrt(ms + RMS_EPS)) * g_ref[...]).astype(BF16)
    col = 0
    for o_ref, width in zip(out_refs, seg_widths):
        for c0 in range(0, width, col_chunk):
            cw = min(col_chunk, width - c0)
            o_ref[:, c0:c0 + cw] = jnp.dot(
                h, w_ref[:, col + c0:col + c0 + cw], preferred_element_type=F32)
        col += width


def _norm_matmul(x2d, g, w_bf16, seg_widths, tm):
    n, d = x2d.shape
    ncol = w_bf16.shape[1]
    assert sum(seg_widths) == ncol and n % tm == 0
    kern = functools.partial(_norm_matmul_kernel, seg_widths=tuple(seg_widths), col_chunk=512)
    return pl.pallas_call(
        kern,
        grid=(n // tm,),
        in_specs=[
            pl.BlockSpec((tm, d), lambda i: (i, 0)),
            pl.BlockSpec((1, d), lambda i: (0, 0)),
            pl.BlockSpec((d, ncol), lambda i: (0, 0), pipeline_mode=pl.Buffered(1)),
        ],
        out_specs=[pl.BlockSpec((tm, w), lambda i: (i, 0)) for w in seg_widths],
        out_shape=[jax.ShapeDtypeStruct((n, w), F32) for w in seg_widths],
        compiler_params=_cparams(1),
        name="norm_matmul",
    )(x2d, g.reshape(1, d), w_bf16)


def _proj_prep_kernel(*refs, has_vres):
    if has_vres:
        (x_ref, g_ref, w_ref, mu_ref, w0_ref, a0_ref, wa2_ref, kk_ref, ka_ref, tri_ref, ones_ref,
         vf_ref, v0_ref, v1_ref, v2_ref,
         r_out, k_out, v_out, na_out, nb_out, lw_out, cs_out, carry_ref) = refs
    else:
        (x_ref, g_ref, w_ref, mu_ref, w0_ref, a0_ref, wa2_ref, kk_ref, ka_ref, tri_ref, ones_ref,
         r_out, k_out, v_out, na_out, nb_out, lw_out, cs_out, carry_ref) = refs

    @pl.when(pl.program_id(1) == 0)
    def _():
        carry_ref[...] = jnp.zeros_like(carry_ref)

    x = x_ref[0]
    tt = x.shape[0]
    ms = jnp.mean(x * x, axis=-1, keepdims=True)
    h = ((x * lax.rsqrt(ms + RMS_EPS)) * g_ref[...]).astype(BF16)
    row = lax.broadcasted_iota(jnp.int32, (tt, 1), 0)

    def shifted(c0, width):
        z = jnp.dot(h, w_ref[:, c0:c0 + width], preferred_element_type=F32)
        prev = jnp.where(row == 0, carry_ref[0:1, c0:c0 + width], pltpu.roll(z, 1, axis=0))
        carry_ref[0:1, c0:c0 + width] = z[tt - 1:tt, :]
        return z + (prev - z) * mu_ref[:, c0:c0 + width]

    xwa = shifted(3 * A_W, 2 * LORA)
    lane = lax.broadcasted_iota(jnp.int32, (1, 2 * LORA), 1)
    lora_in = jnp.where(lane < LORA, jnp.tanh(xwa), xwa)
    lora = _bdot(lora_in, wa2_ref[...])
    wpre = w0_ref[...] + lora[:, 0:A_W]
    neg = -wpre
    softplus = jnp.maximum(neg, 0.0) + jnp.log(1.0 + jnp.exp(-jnp.abs(neg)))
    w_log = -softplus - 0.5
    lw = -jnp.exp(w_log)
    a = _sigmoid(a0_ref[...] + lora[:, A_W:2 * A_W])
    lw_out[0] = lw
    r_out[0] = shifted(0, A_W)
    hi = lw.astype(BF16)
    lo = (lw - hi.astype(F32)).astype(BF16)
    tri = tri_ref[...]
    cs_out[0] = (jnp.dot(tri, hi, preferred_element_type=F32)
                 + jnp.dot(tri, lo, preferred_element_type=F32))
    k = shifted(A_W, A_W)
    k_out[0] = k * (1.0 + (a - 1.0) * ka_ref[...])
    kk = k * kk_ref[...]
    ones_bd = ones_ref[...]
    gw = ones_bd.shape[0]
    for p in range(A_W // gw):
        sl = slice(p * gw, (p + 1) * gw)
        kkp = kk[:, sl]
        ss = _bdot(kkp * kkp, ones_bd)
        kkn = kkp * lax.rsqrt(ss + 1e-12)
        na_out[0, :, sl] = -kkn
        nb_out[0, :, sl] = kkn * a[:, sl]
    v = shifted(2 * A_W, A_W)
    if has_vres:
        vf = vf_ref[0]
        gate = _sigmoid(v0_ref[...] + _bdot(_bdot(v, v1_ref[...]), v2_ref[...]))
        v = v + (vf - v) * gate
    v_out[0] = v


def _proj_prep(x3d, g, w_a, mu, w0, w2, a0, a2, k_k, k_a, vres, tt):
    b, t, d = x3d.shape
    has_vres = vres is not None
    wa2 = jnp.zeros((2 * LORA, 2 * A_W), F32)
    wa2 = wa2.at[:LORA, :A_W].set(w2).at[LORA:, A_W:].set(a2).astype(BF16)
    ti = jnp.arange(tt)
    tri = ((ti[:, None] >= ti[None, :]) &
           (ti[:, None] // WKV_CHUNK == ti[None, :] // WKV_CHUNK)).astype(BF16)
    gw = WKV_GROUP * HEAD
    li = jnp.arange(gw)
    ones_bd = (li[:, None] // HEAD == li[None, :] // HEAD).astype(BF16)
    row = lambda p: p.reshape(1, -1)
    full = lambda shape: pl.BlockSpec(shape, lambda bi, i: (0,) * len(shape))
    tile = lambda w: pl.BlockSpec((1, tt, w), lambda bi, i: (bi, i, 0))
    in_specs = [
        tile(d), full((1, d)),
        pl.BlockSpec((d, SHIFT_W), lambda bi, i: (0, 0), pipeline_mode=pl.Buffered(1)),
        full((1, SHIFT_W)), full((1, A_W)), full((1, A_W)), full((2 * LORA, 2 * A_W)),
        full((1, A_W)), full((1, A_W)), full((tt, tt)), full((gw, gw)),
    ]
    args = [x3d, row(g), w_a, row(mu), row(w0), row(a0), wa2, row(k_k), row(k_a), tri, ones_bd]
    if has_vres:
        v_first, v0, v1, v2 = vres
        in_specs += [tile(A_W), full((1, A_W)), full(v1.shape), full(v2.shape)]
        args += [v_first, row(v0), v1.astype(BF16), v2.astype(BF16)]
    return pl.pallas_call(
        functools.partial(_proj_prep_kernel, has_vres=has_vres),
        grid=(b, t // tt),
        in_specs=in_specs,
        out_specs=[tile(A_W)] * 7,
        out_shape=[jax.ShapeDtypeStruct((b, t, A_W), F32)] * 7,
        scratch_shapes=[pltpu.VMEM((8, SHIFT_W), F32)],
        compiler_params=_cparams(2),
        name="proj_prep",
    )(*args)


def _wkv_kernel(r_ref, k_ref, v_ref, na_ref, nb_ref, lw_ref, cs_ref, g_ref,
                lnw_ref, lnb_ref, rk_ref, ones_ref, o_ref, s_ref):
    c = WKV_CHUNK

    @pl.when(pl.program_id(0) == 0)
    def _():
        s_ref[...] = jnp.zeros_like(s_ref)

    gw = WKV_GROUP * HEAD
    lane = lax.broadcasted_iota(jnp.int32, (1, gw), 1)
    head_masks = [(lane >= h * HEAD) & (lane < (h + 1) * HEAD) for h in range(WKV_GROUP)]
    ti = lax.broadcasted_iota(jnp.int32, (c, gw), 0)
    tj = lax.broadcasted_iota(jnp.int32, (c, gw), 1) & (c - 1)
    strict = ti > tj
    incl = ti >= tj
    eye_t = jnp.where(ti == tj, 1.0, 0.0)
    li = lax.broadcasted_iota(jnp.int32, (gw, gw), 0)
    lj = lax.broadcasted_iota(jnp.int32, (gw, gw), 1)
    bdmask = (li // HEAD) == (lj // HEAD)
    eye_full = li == lj
    ones_bd = ones_ref[...]

    def bd(x):
        xb = x.astype(BF16)
        zero = jnp.zeros_like(xb)
        return jnp.concatenate([jnp.where(m, xb, zero) for m in head_masks], axis=0)

    n_ck = r_ref.shape[1] // c
    n_state = s_ref.shape[0]
    chains = [(bi, slice(ck * c, (ck + 1) * c), slice(p * gw, (p + 1) * gw))
              for ck in range(n_ck) for bi in range(r_ref.shape[0]) for p in range(A_W // gw)]
    pairs = range(len(chains))
    rd = lambda ref, p: ref[chains[p]]
    rt, at_abs, bh, kh, wc, em, lhs0, rhs0 = [], [], [], [], [], [], [], []
    for p in pairs:
        cs = rd(cs_ref, p)
        mid = cs[c // 2 - 1:c // 2, :]
        last = cs[c - 1:c, :]
        e_neg = jnp.exp(mid - cs)
        rt_p = rd(r_ref, p) * jnp.exp(cs - mid)
        at_p = rd(na_ref, p) * jnp.exp(cs - rd(lw_ref, p) - mid)
        kt_p = rd(k_ref, p) * e_neg
        bt_p = rd(nb_ref, p) * e_neg
        em_p = jnp.exp(mid)
        e_end = jnp.exp(last - mid)
        rt.append(rt_p)
        at_abs.append(at_p * em_p)
        em.append(em_p)
        wc.append(jnp.exp(last))
        bh.append(bt_p * e_end)
        kh.append(kt_p * e_end)
        lhs0.append(jnp.concatenate([at_p, rt_p], axis=0))
        rhs0.append(jnp.concatenate([bd(bt_p), bd(kt_p)], axis=0))
    sc = [_bdot_nt(lhs0[p], rhs0[p]) for p in pairs]
    a_ab = [jnp.where(strict, sc[p][:c, :gw], 0.0) for p in pairs]
    a_rb = [jnp.where(incl, sc[p][c:, :gw], 0.0) for p in pairs]
    akrk = [jnp.concatenate([jnp.where(strict, sc[p][:c, gw:], 0.0),
                             jnp.where(incl, sc[p][c:, gw:], 0.0)], axis=0) for p in pairs]
    xv = [_bdot(akrk[p], bd(rd(v_ref, p))) for p in pairs]
    pinv = [eye_t + a_ab[p] for p in pairs]
    apow = [_bdot(a_ab[p], bd(a_ab[p])) for p in pairs]
    for _ in range(4):
        both = [_bdot(jnp.concatenate([apow[p], pinv[p]], axis=0), bd(apow[p])) for p in pairs]
        pinv = [pinv[p] + both[p][c:] for p in pairs]
        apow = [both[p][:c] for p in pairs]
    pinv = [pinv[p] + _bdot(pinv[p], bd(apow[p])) for p in pairs]
    ta = [_bdot(pinv[p], jnp.concatenate([bd(xv[p][:c]), bd(at_abs[p])], axis=1)) for p in pairs]
    uv = [ta[p][:, :gw] for p in pairs]
    ap = [ta[p][:, gw:] for p in pairs]
    rb = [_bdot(a_rb[p], jnp.concatenate([bd(ap[p]), bd(uv[p])], axis=1)) for p in pairs]
    m_gram = [_bdot_tn(ap[p], bh[p]) for p in pairs]
    n_gram = [_bdot_tn(jnp.concatenate([uv[p], rd(v_ref, p)], axis=0),
                       jnp.concatenate([bh[p], kh[p]], axis=0)) for p in pairs]
    rp = [rt[p] * em[p] + rb[p][:, :gw] for p in pairs]
    yv = [rb[p][:, gw:] + xv[p][c:] for p in pairs]
    m_mat = [jnp.where(eye_full, wc[p], 0.0) + jnp.where(bdmask, m_gram[p], 0.0) for p in pairs]
    n_mat = [jnp.where(bdmask, n_gram[p], 0.0) for p in pairs]
    state = [s_ref[si] for si in range(n_state)]
    y = []
    for ck in range(n_ck):
        ps = [ck * n_state + si for si in range(n_state)]
        y += [_bdot_nt(rp[p], state[si]) + yv[p] for si, p in enumerate(ps)]
        state = [_bdot(state[si], m_mat[p]) + n_mat[p] for si, p in enumerate(ps)]
    for si in range(n_state):
        s_ref[si] = state[si]
    stat_rows = []
    for p in pairs:
        stat_rows += [y[p], rd(r_ref, p) * rd(k_ref, p) * rk_ref[:, chains[p][2]]]
    stats_all = _bdot(jnp.concatenate(stat_rows, axis=0), ones_bd)
    stats = [stats_all[2 * c * p:2 * c * (p + 1)] for p in pairs]
    d = [y[p] - stats[p][:c] * (1.0 / HEAD) for p in pairs]
    var_all = _bdot(jnp.concatenate([d[p] * d[p] for p in pairs], axis=0), ones_bd) * (1.0 / HEAD)
    var = [var_all[c * p:c * (p + 1)] for p in pairs]
    for p in pairs:
        sl = chains[p][2]
        yn = d[p] * lax.rsqrt(var[p] + LN_X_EPS) * lnw_ref[:, sl] + lnb_ref[:, sl]
        out = yn + stats[p][c:] * rd(v_ref, p)
        o_ref[chains[p]] = out * _silu(rd(g_ref, p))


def _wkv(r, k, v, na, nb, lw, cs, gate, ln_w, ln_b, r_k):
    b, t, _ = r.shape
    c = WKV_CHUNK
    gw = WKV_GROUP * HEAD
    li = jnp.arange(gw)
    ones_bd = (li[:, None] // HEAD == li[None, :] // HEAD).astype(BF16)
    rows = WKV_STEP_CHUNKS * c
    assert t % rows == 0
    tile = pl.BlockSpec((b, rows, A_W), lambda i: (0, i, 0))
    vec = pl.BlockSpec((1, A_W), lambda i: (0, 0))
    return pl.pallas_call(
        _wkv_kernel,
        grid=(t // rows,),
        in_specs=[tile] * 8 + [vec] * 3 + [pl.BlockSpec((gw, gw), lambda i: (0, 0))],
        out_specs=tile,
        out_shape=jax.ShapeDtypeStruct((b, t, A_W), F32),
        scratch_shapes=[pltpu.VMEM((b * (A_W // gw), gw, gw), F32)],
        compiler_params=_cparams(1),
        name="wkv",
    )(r, k, v, na, nb, lw, cs, gate, ln_w.reshape(1, A_W), ln_b.reshape(1, A_W),
      r_k.reshape(1, A_W), ones_bd)


def _rope_tables_kernel(pos_ref, freq_ref, sgn_ref, cos_ref, sin_ref):
    ang = pos_ref[0].astype(F32) * freq_ref[...]
    cos_ref[0] = jnp.cos(ang)
    sin_ref[0] = jnp.sin(ang) * sgn_ref[...]


def _rope_tables(positions, rows):
    b, t = positions.shape
    half = ROPE_DIM // 2
    li = jnp.arange(LANES) % HEAD
    inv_freq = jnp.power(jnp.float32(ROPE_THETA), -jnp.arange(half, dtype=F32) / half)
    freq = jnp.where(li < ROPE_DIM, inv_freq[li % half], 0.0).reshape(1, LANES).astype(F32)
    sgn = jnp.where(li < half, -1.0, jnp.where(li < ROPE_DIM, 1.0, 0.0)).reshape(1, LANES).astype(F32)
    full = pl.BlockSpec((1, LANES), lambda bi, i: (0, 0))
    tab = pl.BlockSpec((1, rows, LANES), lambda bi, i: (bi, i, 0))
    return pl.pallas_call(
        _rope_tables_kernel,
        grid=(b, t // rows),
        in_specs=[pl.BlockSpec((1, rows, 1), lambda bi, i: (bi, i, 0)), full, full],
        out_specs=[tab, tab],
        out_shape=[jax.ShapeDtypeStruct((b, t, LANES), F32)] * 2,
        compiler_params=_cparams(2),
        name="rope_tables",
    )(positions.reshape(b, t, 1), freq, sgn)


def _rope(x, cos_t, sin_t, first8):
    fwd = pltpu.roll(x, LANES - ROPE_DIM // 2, axis=1)
    bwd = pltpu.roll(x, ROPE_DIM // 2, axis=1)
    return x * cos_t + jnp.where(first8, fwd, bwd) * sin_t


def _dup_head(tile, which, m0):
    swapped = pltpu.roll(tile, HEAD, axis=1)
    if which == 0:
        return jnp.where(m0, tile, swapped)
    return jnp.where(m0, swapped, tile)


def _swa_kernel(q_ref, kvc_ref, kvp_ref, cosc_ref, sinc_ref, cosp_ref, sinp_ref, g_ref, sink_ref,
                o_ref):
    n = pl.program_id(1)
    blk = WINDOW
    m0, m1 = _head_masks()
    lane = lax.broadcasted_iota(jnp.int32, (1, LANES), 1)
    first8 = (lane & (HEAD - 1)) < ROPE_DIM // 2
    cos_c, sin_c = cosc_ref[0], sinc_ref[0]
    cos_p, sin_p = cosp_ref[0], sinp_ref[0]
    qi = lax.broadcasted_iota(jnp.int32, (blk, 2 * blk), 0) + blk
    ki = lax.broadcasted_iota(jnp.int32, (blk, 2 * blk), 1)
    valid = (ki <= qi) & (qi - ki < WINDOW) & ((n > 0) | (ki >= blk))
    scale = 1.0 / math.sqrt(HEAD)
    ri = lax.broadcasted_iota(jnp.int32, (4 * blk, LANES), 0)
    ci = lax.broadcasted_iota(jnp.int32, (4 * blk, LANES), 1)
    sum_cols = jnp.where((ri >= 2 * blk) == (ci >= HEAD), 1.0, 0.0).astype(BF16)
    tile = lambda j: slice(j * LANES, (j + 1) * LANES)
    k2s, rhs_pv = [], []
    for kt in range(B_KVW // LANES):
        k_tile = jnp.concatenate(
            [_rope(kvp_ref[0, :, tile(kt)], cos_p, sin_p, first8),
             _rope(kvc_ref[0, :, tile(kt)], cos_c, sin_c, first8)], axis=0)
        v_tile = jnp.concatenate(
            [kvp_ref[0, :, B_KVW + kt * LANES:B_KVW + (kt + 1) * LANES],
             kvc_ref[0, :, B_KVW + kt * LANES:B_KVW + (kt + 1) * LANES]], axis=0)
        for which in range(2):
            k2s.append(_dup_head(k_tile, which, m0).astype(BF16))
            v2 = _dup_head(v_tile, which, m0)
            vstack = jnp.concatenate([jnp.where(m0, v2, 0.0), jnp.where(m1, v2, 0.0)], axis=0)
            rhs_pv.append(jnp.concatenate([vstack.astype(BF16), sum_cols], axis=1))
    scores = []
    for g in range(B_KVW // HEAD):
        rows = []
        for pr in range(B_GROUP // 2):
            qp = _rope(q_ref[0, :, tile(2 * g + pr)], cos_c, sin_c, first8) * scale
            rows += [jnp.where(m0, qp, 0.0), jnp.where(m1, qp, 0.0)]
        scores.append(_bdot_nt(jnp.concatenate(rows, axis=0), k2s[g]))
    for g in range(B_KVW // HEAD):
        for pr in range(B_GROUP // 2):
            qt = 2 * g + pr
            probs, sink_terms = [], []
            for half in range(2):
                r0 = (2 * pr + half) * blk
                s = jnp.where(valid, scores[g][r0:r0 + blk], -jnp.inf)
                sink = sink_ref[:, qt * LANES + half * HEAD:qt * LANES + half * HEAD + 1]
                m = jnp.maximum(jnp.max(s, axis=-1, keepdims=True), sink)
                probs.append(jnp.exp(s - m).astype(BF16))
                sink_terms.append(jnp.exp(sink - m))
            od = jnp.dot(jnp.concatenate(probs, axis=1), rhs_pv[g], preferred_element_type=F32)
            den = od[:, LANES:] + jnp.where(m0, sink_terms[0], sink_terms[1])
            o_ref[0, :, tile(qt)] = od[:, :LANES] * (1.0 / den) * _silu(g_ref[0, :, tile(qt)])


def _swa(q, kv, cos_t, sin_t, gate, sinks):
    b, t, _ = q.shape
    blk = WINDOW
    sink_lanes = jnp.repeat(sinks.astype(F32), HEAD).reshape(1, B_W)
    cur = lambda w: pl.BlockSpec((1, blk, w), lambda bi, i: (bi, i, 0))
    prev = lambda w: pl.BlockSpec((1, blk, w), lambda bi, i: (bi, jnp.maximum(i - 1, 0), 0))
    return pl.pallas_call(
        _swa_kernel,
        grid=(b, t // blk),
        in_specs=[cur(B_W), cur(2 * B_KVW), prev(2 * B_KVW), cur(LANES), cur(LANES), prev(LANES),
                  prev(LANES), cur(B_W), pl.BlockSpec((1, B_W), lambda bi, i: (0, 0))],
        out_specs=cur(B_W),
        out_shape=jax.ShapeDtypeStruct((b, t, B_W), F32),
        compiler_params=_cparams(2),
        name="swa",
    )(q, kv, kv, cos_t, sin_t, cos_t, sin_t, gate, sink_lanes)


def _out_proj_kernel(x_ref, ya_ref, yb_ref, w_ref, o_ref):
    y = jnp.concatenate([ya_ref[...].astype(BF16), yb_ref[...].astype(BF16)], axis=1)
    o_ref[...] = x_ref[...] + jnp.dot(y, w_ref[...], preferred_element_type=F32)


def _out_proj(x2d, ya, yb, w_bf16, tm):
    n, d = x2d.shape
    return pl.pallas_call(
        _out_proj_kernel,
        grid=(n // tm,),
        in_specs=[
            pl.BlockSpec((tm, d), lambda i: (i, 0)),
            pl.BlockSpec((tm, A_W), lambda i: (i, 0)),
            pl.BlockSpec((tm, B_W), lambda i: (i, 0)),
            pl.BlockSpec((A_W + B_W, d), lambda i: (0, 0), pipeline_mode=pl.Buffered(1)),
        ],
        out_specs=pl.BlockSpec((tm, d), lambda i: (i, 0)),
        out_shape=jax.ShapeDtypeStruct((n, d), F32),
        compiler_params=_cparams(1),
        name="out_proj",
    )(x2d, ya, yb, w_bf16)


def _odd_layer_kernel(x_ref, g_ref, win_ref, lnw_ref, lnb_ref, ws_ref, bst_ref, wout_ref,
                      fin_ref, o_ref, y_ref, *, final_norm):
    tm, d = x_ref.shape
    ch = SGU_CHUNK
    x = x_ref[...]
    ms = jnp.mean(x * x, axis=-1, keepdims=True)
    h = ((x * lax.rsqrt(ms + RMS_EPS)) * g_ref[...]).astype(BF16)
    v = jnp.dot(h, win_ref[:, d:2 * d], preferred_element_type=F32)
    mean = jnp.mean(v, axis=-1, keepdims=True)
    dv = v - mean
    var = jnp.mean(dv * dv, axis=-1, keepdims=True)
    vn = ((dv * lax.rsqrt(var + LN_EPS)) * lnw_ref[...] + lnb_ref[...]).astype(BF16)
    ti = lax.broadcasted_iota(jnp.int32, (ch, ch), 0)
    si = lax.broadcasted_iota(jnp.int32, (ch, ch), 1)
    causal = ti >= si
    cw = 2 * LANES
    zero_tile = jnp.zeros((ch, LANES), BF16)
    for j in range(d // cw):
        c0 = j * cw
        u = jnp.dot(h, win_ref[:, c0:c0 + cw], preferred_element_type=F32)
        gate = jnp.dot(h, win_ref[:, 2 * d + c0:2 * d + c0 + cw], preferred_element_type=F32)
        wm = jnp.concatenate(
            [jnp.where(causal, ws_ref[2 * j + gi], 0.0).astype(BF16) for gi in range(2)], axis=1)
        bias = jnp.concatenate(
            [jnp.broadcast_to(bst_ref[:, 2 * j + gi:2 * j + gi + 1], (ch, LANES)) for gi in range(2)],
            axis=1)
        mixed = []
        for ci in range(tm // ch):
            vc = vn[ci * ch:(ci + 1) * ch, c0:c0 + cw]
            v_bd = jnp.concatenate(
                [jnp.concatenate([vc[:, :LANES], zero_tile], axis=1),
                 jnp.concatenate([zero_tile, vc[:, LANES:]], axis=1)], axis=0)
            mixed.append(jnp.dot(wm, v_bd, preferred_element_type=F32) + bias)
        y_ref[:, c0:c0 + cw] = (u * jnp.concatenate(mixed, axis=0) * _silu(gate)).astype(BF16)
    out = x + jnp.dot(y_ref[...], wout_ref[...], preferred_element_type=F32)
    if final_norm:
        ms = jnp.mean(out * out, axis=-1, keepdims=True)
        out = (out * lax.rsqrt(ms + RMS_EPS)) * fin_ref[...]
    o_ref[...] = out


def _odd_layer(x2d, g, w_in_bf16, ln_w, ln_b, ws, bs, w_out_bf16, final_g, tm):
    n, d = x2d.shape
    final_norm = final_g is not None
    fin = final_g if final_norm else jnp.ones((d,), F32)
    row_tile = pl.BlockSpec((tm, d), lambda i: (i, 0))
    vec = pl.BlockSpec((1, d), lambda i: (0, 0))
    return pl.pallas_call(
        functools.partial(_odd_layer_kernel, final_norm=final_norm),
        grid=(n // tm,),
        in_specs=[
            row_tile, vec,
            pl.BlockSpec((d, 3 * d), lambda i: (0, 0), pipeline_mode=pl.Buffered(1)),
            vec, vec,
            pl.BlockSpec((SGU_GROUPS, SGU_CHUNK, SGU_CHUNK), lambda i: (0, 0, 0)),
            pl.BlockSpec((SGU_CHUNK, SGU_GROUPS), lambda i: (0, 0)),
            pl.BlockSpec((d, d), lambda i: (0, 0), pipeline_mode=pl.Buffered(1)),
            vec,
        ],
        out_specs=row_tile,
        out_shape=jax.ShapeDtypeStruct((n, d), F32),
        scratch_shapes=[pltpu.VMEM((tm, d), BF16)],
        compiler_params=_cparams(1),
        name="odd_layer",
    )(x2d, g.reshape(1, d), w_in_bf16, ln_w.reshape(1, d), ln_b.reshape(1, d), ws, bs.T,
      w_out_bf16, fin.reshape(1, d))


def kernel(x, positions, e_norm, e_w_in, e_mu, rwkv_w0, rwkv_w2, rwkv_a0, rwkv_a2, rwkv_k_k, rwkv_k_a, rwkv_r_k, rwkv_ln_w, rwkv_ln_b, rwkv_v0, rwkv_v1, rwkv_v2, attn_sinks, e_w_out, o_norm, o_w_in, sgu_ln_w, sgu_ln_b, sgu_ws, sgu_bs, o_w_out, final_norm):
    b, t, d = x.shape
    n = b * t
    depth = e_norm.shape[0] + o_norm.shape[0]
    assert t % 256 == 0 and d == D_MODEL and depth % 2 == 0
    x2d = x.reshape(n, d)
    cos_t, sin_t = _rope_tables(positions, rows=256)
    v_first = None
    for layer in range(depth):
        if layer % 2 == 0:
            e = layer // 2
            w_in = e_w_in[e].astype(BF16)
            r3 = lambda z: z.reshape(b, t, z.shape[-1])
            vres = None if e == 0 else (v_first, rwkv_v0[e - 1], rwkv_v1[e - 1], rwkv_v2[e - 1])
            r, k2, v, na, nb, lw, cs = _proj_prep(
                r3(x2d), e_norm[e], w_in[:, :SHIFT_W], e_mu[e], rwkv_w0[e], rwkv_w2[e], rwkv_a0[e],
                rwkv_a2[e], rwkv_k_k[e], rwkv_k_a[e], vres, tt=256)
            a_gate, q, kv, b_gate = _norm_matmul(
                x2d, e_norm[e], w_in[:, SHIFT_W:], (A_W, B_W, 2 * B_KVW, B_W), tm=512)
            if e == 0:
                v_first = v
            ya = _wkv(r, k2, v, na, nb, lw, cs, r3(a_gate), rwkv_ln_w[e], rwkv_ln_b[e], rwkv_r_k[e])
            yb = _swa(r3(q), r3(kv), cos_t, sin_t, r3(b_gate), attn_sinks[e])
            x2d = _out_proj(x2d, ya.reshape(n, A_W), yb.reshape(n, B_W), e_w_out[e].astype(BF16), tm=512)
        else:
            o = layer // 2
            fin = final_norm if layer == depth - 1 else None
            x2d = _odd_layer(x2d, o_norm[o], o_w_in[o].astype(BF16), sgu_ln_w[o], sgu_ln_b[o],
                             sgu_ws[o], sgu_bs[o], o_w_out[o].astype(BF16), fin, tm=256)
    return x2d.reshape(b, t, d)
```

```python
import functools
import math

import jax
import jax.numpy as jnp
from jax import lax
from jax.experimental import pallas as pl
from jax.experimental.pallas import tpu as pltpu

F32 = jnp.float32
BF16 = jnp.bfloat16

D_MODEL = 2048
HEAD = 64
A_W = 1024
B_W = 1024
B_KVW = 256
B_GROUP = 4
LORA = 64
SHIFT_W = 3 * A_W + 2 * LORA
WINDOW = 128
ROPE_DIM = 16
ROPE_THETA = 500000.0
SGU_CHUNK = 128
SGU_GROUPS = 16
RMS_EPS = 1e-5
LN_EPS = 1e-5
LN_X_EPS = HEAD * 1e-5

LANES = 128
WKV_CHUNK = 64
WKV_GROUP = 4
WKV_STEP_CHUNKS = 2
VMEM_LIMIT = 56 * 1024 * 1024


def _cparams(n_axes):
    return pltpu.CompilerParams(
        dimension_semantics=("arbitrary",) * n_axes, vmem_limit_bytes=VMEM_LIMIT)


def _bdot(a, b):
    return jnp.dot(a.astype(BF16), b.astype(BF16), preferred_element_type=F32)


def _bdot_nt(a, b):
    return lax.dot_general(a.astype(BF16), b.astype(BF16), (((1,), (1,)), ((), ())),
                           preferred_element_type=F32)


def _bdot_tn(a, b):
    return lax.dot_general(a.astype(BF16), b.astype(BF16), (((0,), (0,)), ((), ())),
                           preferred_element_type=F32)


def _split_dot(x, ones_bf16, passes):
    acc = None
    rem = x
    for _ in range(passes):
        piece = rem.astype(BF16)
        term = jnp.dot(piece, ones_bf16, preferred_element_type=F32)
        acc = term if acc is None else acc + term
        rem = rem - piece.astype(F32)
    return acc


def _silu(x):
    return x * (1.0 / (1.0 + jnp.exp(-x)))


def _sigmoid(x):
    return 1.0 / (1.0 + jnp.exp(-x))


def _head_masks():
    lane = lax.broadcasted_iota(jnp.int32, (1, LANES), 1)
    return lane < HEAD, lane >= HEAD


def _norm_matmul_kernel(x_ref, g_ref, w_ref, *out_refs, seg_widths, col_chunk):
    x = x_ref[...]
    ms = jnp.mean(x * x, axis=-1, keepdims=True)
    h = ((x * lax.rsqrt(ms + RMS_EPS)) * g_ref[...]).astype(BF16)
    col = 0
    for o_ref, width in zip(out_refs, seg_widths):
        for c0 in range(0, width, col_chunk):
            cw = min(col_chunk, width - c0)
            o_ref[:, c0:c0 + cw] = jnp.dot(
                h, w_ref[:, col + c0:col + c0 + cw], preferred_element_type=F32)
        col += width


def _norm_matmul(x2d, g, w_bf16, seg_widths, tm):
    n, d = x2d.shape
    ncol = w_bf16.shape[1]
    assert sum(seg_widths) == ncol and n % tm == 0
    kern = functools.partial(_norm_matmul_kernel, seg_widths=tuple(seg_widths), col_chunk=512)
    return pl.pallas_call(
        kern,
        grid=(n // tm,),
        in_specs=[
            pl.BlockSpec((tm, d), lambda i: (i, 0)),
            pl.BlockSpec((1, d), lambda i: (0, 0)),
            pl.BlockSpec((d, ncol), lambda i: (0, 0), pipeline_mode=pl.Buffered(1)),
        ],
        out_specs=[pl.BlockSpec((tm, w), lambda i: (i, 0)) for w in seg_widths],
        out_shape=[jax.ShapeDtypeStruct((n, w), F32) for w in seg_widths],
        compiler_params=_cparams(1),
        name="norm_matmul",
    )(x2d, g.reshape(1, d), w_bf16)


def _prep_swa_kernel(*refs, has_vres):
    (x_ref, g_ref, w_ref, mu_ref, w0_ref, a0_ref, wa2_ref, kk_ref, ka_ref, tri_ref, ones_ref) = refs[:11]
    if has_vres:
        vf_ref, v0_ref, v1_ref, v2_ref = refs[11:15]
    (q_ref, kvc_ref, kvp_ref, cosc_ref, sinc_ref, cosp_ref, sinp_ref, bg_ref, sink_ref,
     r_out, k_out, v_out, na_out, nb_out, lw_out, cs_out, yb_out, carry_ref) = refs[-18:]

    @pl.when(pl.program_id(1) == 0)
    def _():
        carry_ref[...] = jnp.zeros_like(carry_ref)

    x = x_ref[0]
    tt = x.shape[0]

    consts = _swa_consts()
    first8 = consts[2]
    blk = WINDOW
    n_blk = tt // blk
    k_roped, v_rows = [], []
    for j in range(n_blk + 1):
        if j == 0:
            kv, cos_j, sin_j = kvp_ref[0], cosp_ref[0], sinp_ref[0]
        else:
            rows = slice((j - 1) * blk, j * blk)
            kv, cos_j, sin_j = kvc_ref[0, rows], cosc_ref[0, rows], sinc_ref[0, rows]
        k_roped.append([_rope(kv[:, _lane_tile(kt)], cos_j, sin_j, first8)
                        for kt in range(B_KVW // LANES)])
        v_rows.append([kv[:, B_KVW + kt * LANES:B_KVW + (kt + 1) * LANES]
                       for kt in range(B_KVW // LANES)])
    attn = []
    for j in range(n_blk):
        rows = slice(j * blk, (j + 1) * blk)
        k_win = [jnp.concatenate([k_roped[j][kt], k_roped[j + 1][kt]], axis=0)
                 for kt in range(B_KVW // LANES)]
        v_win = [jnp.concatenate([v_rows[j][kt], v_rows[j + 1][kt]], axis=0)
                 for kt in range(B_KVW // LANES)]
        attn.append(_swa_scores(lambda qt, rows=rows: q_ref[0, rows, _lane_tile(qt)], k_win, v_win,
                                cosc_ref[0, rows], sinc_ref[0, rows], consts))

    ms = jnp.mean(x * x, axis=-1, keepdims=True)
    h = ((x * lax.rsqrt(ms + RMS_EPS)) * g_ref[...]).astype(BF16)
    row = lax.broadcasted_iota(jnp.int32, (tt, 1), 0)

    def shifted(c0, width):
        z = jnp.dot(h, w_ref[:, c0:c0 + width], preferred_element_type=F32)
        prev = jnp.where(row == 0, carry_ref[0:1, c0:c0 + width], pltpu.roll(z, 1, axis=0))
        carry_ref[0:1, c0:c0 + width] = z[tt - 1:tt, :]
        return z + (prev - z) * mu_ref[:, c0:c0 + width]

    xwa = shifted(3 * A_W, 2 * LORA)
    lane = lax.broadcasted_iota(jnp.int32, (1, 2 * LORA), 1)
    lora_in = jnp.where(lane < LORA, jnp.tanh(xwa), xwa)
    lora = _bdot(lora_in, wa2_ref[...])
    wpre = w0_ref[...] + lora[:, 0:A_W]
    neg = -wpre
    softplus = jnp.maximum(neg, 0.0) + jnp.log(1.0 + jnp.exp(-jnp.abs(neg)))
    w_log = -softplus - 0.5
    lw = -jnp.exp(w_log)
    a = _sigmoid(a0_ref[...] + lora[:, A_W:2 * A_W])
    lw_out[0] = lw
    r_out[0] = shifted(0, A_W)
    hi = lw.astype(BF16)
    lo = (lw - hi.astype(F32)).astype(BF16)
    tri = tri_ref[...]
    cs_out[0] = (jnp.dot(tri, hi, preferred_element_type=F32)
                 + jnp.dot(tri, lo, preferred_element_type=F32))
    k = shifted(A_W, A_W)
    k_out[0] = k * (1.0 + (a - 1.0) * ka_ref[...])
    kk = k * kk_ref[...]
    ones_bd = ones_ref[...]
    gw = ones_bd.shape[0]
    for p in range(A_W // gw):
        sl = slice(p * gw, (p + 1) * gw)
        kkp = kk[:, sl]
        ss = _bdot(kkp * kkp, ones_bd)
        kkn = kkp * lax.rsqrt(ss + 1e-12)
        na_out[0, :, sl] = -kkn
        nb_out[0, :, sl] = kkn * a[:, sl]
    v = shifted(2 * A_W, A_W)
    if has_vres:
        vf = vf_ref[0]
        gate = _sigmoid(v0_ref[...] + _bdot(_bdot(v, v1_ref[...]), v2_ref[...]))
        v = v + (vf - v) * gate
    v_out[0] = v

    for j in range(n_blk):
        rows = slice(j * blk, (j + 1) * blk)
        has_prev = (pl.program_id(1) > 0) if j == 0 else True

        def store_tile(qt, val, rows=rows):
            yb_out[0, rows, _lane_tile(qt)] = val

        _swa_outputs(attn[j][0], attn[j][1], has_prev, sink_ref,
                     lambda qt, rows=rows: bg_ref[0, rows, _lane_tile(qt)], store_tile, consts)


def _prep_swa(x3d, g, w_a, mu, w0, w2, a0, a2, k_k, k_a, vres, q, kv, cos_t, sin_t, b_gate, sinks, tt):
    b, t, d = x3d.shape
    has_vres = vres is not None
    wa2 = jnp.zeros((2 * LORA, 2 * A_W), F32)
    wa2 = wa2.at[:LORA, :A_W].set(w2).at[LORA:, A_W:].set(a2).astype(BF16)
    ti = jnp.arange(tt)
    tri = ((ti[:, None] >= ti[None, :]) &
           (ti[:, None] // WKV_CHUNK == ti[None, :] // WKV_CHUNK)).astype(BF16)
    gw = WKV_GROUP * HEAD
    li = jnp.arange(gw)
    ones_bd = (li[:, None] // HEAD == li[None, :] // HEAD).astype(BF16)
    row = lambda p: p.reshape(1, -1)
    full = lambda shape: pl.BlockSpec(shape, lambda bi, i: (0,) * len(shape))
    tile = lambda w: pl.BlockSpec((1, tt, w), lambda bi, i: (bi, i, 0))
    in_specs = [
        tile(d), full((1, d)),
        pl.BlockSpec((d, SHIFT_W), lambda bi, i: (0, 0), pipeline_mode=pl.Buffered(1)),
        full((1, SHIFT_W)), full((1, A_W)), full((1, A_W)), full((2 * LORA, 2 * A_W)),
        full((1, A_W)), full((1, A_W)), full((tt, tt)), full((gw, gw)),
    ]
    args = [x3d, row(g), w_a, row(mu), row(w0), row(a0), wa2, row(k_k), row(k_a), tri, ones_bd]
    if has_vres:
        v_first, v0, v1, v2 = vres
        in_specs += [tile(A_W), full((1, A_W)), full(v1.shape), full(v2.shape)]
        args += [v_first, row(v0), v1.astype(BF16), v2.astype(BF16)]
    blk = WINDOW
    prev_blk = lambda w: pl.BlockSpec(
        (1, blk, w), lambda bi, i: (bi, jnp.maximum(i * (tt // blk) - 1, 0), 0))
    in_specs += [tile(B_W), tile(2 * B_KVW), prev_blk(2 * B_KVW), tile(LANES), tile(LANES),
                 prev_blk(LANES), prev_blk(LANES), tile(B_W), full((1, B_W))]
    args += [q, kv, kv, cos_t, sin_t, cos_t, sin_t, b_gate,
             jnp.repeat(sinks.astype(F32), HEAD).reshape(1, B_W)]
    return pl.pallas_call(
        functools.partial(_prep_swa_kernel, has_vres=has_vres),
        grid=(b, t // tt),
        in_specs=in_specs,
        out_specs=[tile(A_W)] * 7 + [tile(B_W)],
        out_shape=[jax.ShapeDtypeStruct((b, t, A_W), F32)] * 7 + [jax.ShapeDtypeStruct((b, t, B_W), F32)],
        scratch_shapes=[pltpu.VMEM((8, SHIFT_W), F32)],
        compiler_params=_cparams(2),
        name="prep_swa",
    )(*args)


def _wkv_kernel(r_ref, k_ref, v_ref, na_ref, nb_ref, lw_ref, cs_ref, g_ref,
                lnw_ref, lnb_ref, rk_ref, ones_ref, o_ref, s_ref):
    c = WKV_CHUNK

    @pl.when(pl.program_id(0) == 0)
    def _():
        s_ref[...] = jnp.zeros_like(s_ref)

    gw = WKV_GROUP * HEAD
    lane = lax.broadcasted_iota(jnp.int32, (1, gw), 1)
    head_masks = [(lane >= h * HEAD) & (lane < (h + 1) * HEAD) for h in range(WKV_GROUP)]
    ti = lax.broadcasted_iota(jnp.int32, (c, gw), 0)
    tj = lax.broadcasted_iota(jnp.int32, (c, gw), 1) & (c - 1)
    strict = ti > tj
    incl = ti >= tj
    eye_t = jnp.where(ti == tj, 1.0, 0.0)
    li = lax.broadcasted_iota(jnp.int32, (gw, gw), 0)
    lj = lax.broadcasted_iota(jnp.int32, (gw, gw), 1)
    bdmask = (li // HEAD) == (lj // HEAD)
    eye_full = li == lj
    ones_bd = ones_ref[...]

    def bd(x):
        xb = x.astype(BF16)
        zero = jnp.zeros_like(xb)
        return jnp.concatenate([jnp.where(m, xb, zero) for m in head_masks], axis=0)

    n_ck = r_ref.shape[1] // c
    n_state = s_ref.shape[0]
    chains = [(bi, slice(ck * c, (ck + 1) * c), slice(p * gw, (p + 1) * gw))
              for ck in range(n_ck) for bi in range(r_ref.shape[0]) for p in range(A_W // gw)]
    pairs = range(len(chains))
    rd = lambda ref, p: ref[chains[p]]
    rt, at_abs, bh, kh, wc, em, lhs0, rhs0 = [], [], [], [], [], [], [], []
    for p in pairs:
        cs = rd(cs_ref, p)
        mid = cs[c // 2 - 1:c // 2, :]
        last = cs[c - 1:c, :]
        e_neg = jnp.exp(mid - cs)
        rt_p = rd(r_ref, p) * jnp.exp(cs - mid)
        at_p = rd(na_ref, p) * jnp.exp(cs - rd(lw_ref, p) - mid)
        kt_p = rd(k_ref, p) * e_neg
        bt_p = rd(nb_ref, p) * e_neg
        em_p = jnp.exp(mid)
        e_end = jnp.exp(last - mid)
        rt.append(rt_p)
        at_abs.append(at_p * em_p)
        em.append(em_p)
        wc.append(jnp.exp(last))
        bh.append(bt_p * e_end)
        kh.append(kt_p * e_end)
        lhs0.append(jnp.concatenate([at_p, rt_p], axis=0))
        rhs0.append(jnp.concatenate([bd(bt_p), bd(kt_p)], axis=0))
    sc = [_bdot_nt(lhs0[p], rhs0[p]) for p in pairs]
    a_ab = [jnp.where(strict, sc[p][:c, :gw], 0.0) for p in pairs]
    a_rb = [jnp.where(incl, sc[p][c:, :gw], 0.0) for p in pairs]
    akrk = [jnp.concatenate([jnp.where(strict, sc[p][:c, gw:], 0.0),
                             jnp.where(incl, sc[p][c:, gw:], 0.0)], axis=0) for p in pairs]
    xv = [_bdot(akrk[p], bd(rd(v_ref, p))) for p in pairs]
    pinv = [eye_t + a_ab[p] for p in pairs]
    apow = [_bdot(a_ab[p], bd(a_ab[p])) for p in pairs]
    for _ in range(4):
        both = [_bdot(jnp.concatenate([apow[p], pinv[p]], axis=0), bd(apow[p])) for p in pairs]
        pinv = [pinv[p] + both[p][c:] for p in pairs]
        apow = [both[p][:c] for p in pairs]
    pinv = [pinv[p] + _bdot(pinv[p], bd(apow[p])) for p in pairs]
    ta = [_bdot(pinv[p], jnp.concatenate([bd(xv[p][:c]), bd(at_abs[p])], axis=1)) for p in pairs]
    uv = [ta[p][:, :gw] for p in pairs]
    ap = [ta[p][:, gw:] for p in pairs]
    rb = [_bdot(a_rb[p], jnp.concatenate([bd(ap[p]), bd(uv[p])], axis=1)) for p in pairs]
    m_gram = [_bdot_tn(ap[p], bh[p]) for p in pairs]
    n_gram = [_bdot_tn(jnp.concatenate([uv[p], rd(v_ref, p)], axis=0),
                       jnp.concatenate([bh[p], kh[p]], axis=0)) for p in pairs]
    rp = [rt[p] * em[p] + rb[p][:, :gw] for p in pairs]
    yv = [rb[p][:, gw:] + xv[p][c:] for p in pairs]
    m_mat = [jnp.where(eye_full, wc[p], 0.0) + jnp.where(bdmask, m_gram[p], 0.0) for p in pairs]
    n_mat = [jnp.where(bdmask, n_gram[p], 0.0) for p in pairs]
    state = [s_ref[si] for si in range(n_state)]
    y = []
    for ck in range(n_ck):
        ps = [ck * n_state + si for si in range(n_state)]
        y += [_bdot_nt(rp[p], state[si]) + yv[p] for si, p in enumerate(ps)]
        state = [_bdot(state[si], m_mat[p]) + n_mat[p] for si, p in enumerate(ps)]
    for si in range(n_state):
        s_ref[si] = state[si]
    stat_rows = []
    for p in pairs:
        stat_rows += [y[p], rd(r_ref, p) * rd(k_ref, p) * rk_ref[:, chains[p][2]]]
    stats_all = _bdot(jnp.concatenate(stat_rows, axis=0), ones_bd)
    stats = [stats_all[2 * c * p:2 * c * (p + 1)] for p in pairs]
    d = [y[p] - stats[p][:c] * (1.0 / HEAD) for p in pairs]
    var_all = _bdot(jnp.concatenate([d[p] * d[p] for p in pairs], axis=0), ones_bd) * (1.0 / HEAD)
    var = [var_all[c * p:c * (p + 1)] for p in pairs]
    for p in pairs:
        sl = chains[p][2]
        yn = d[p] * lax.rsqrt(var[p] + LN_X_EPS) * lnw_ref[:, sl] + lnb_ref[:, sl]
        out = yn + stats[p][c:] * rd(v_ref, p)
        o_ref[chains[p]] = out * _silu(rd(g_ref, p))


def _wkv(r, k, v, na, nb, lw, cs, gate, ln_w, ln_b, r_k):
    b, t, _ = r.shape
    c = WKV_CHUNK
    gw = WKV_GROUP * HEAD
    li = jnp.arange(gw)
    ones_bd = (li[:, None] // HEAD == li[None, :] // HEAD).astype(BF16)
    rows = WKV_STEP_CHUNKS * c
    assert t % rows == 0
    tile = pl.BlockSpec((b, rows, A_W), lambda i: (0, i, 0))
    vec = pl.BlockSpec((1, A_W), lambda i: (0, 0))
    return pl.pallas_call(
        _wkv_kernel,
        grid=(t // rows,),
        in_specs=[tile] * 8 + [vec] * 3 + [pl.BlockSpec((gw, gw), lambda i: (0, 0))],
        out_specs=tile,
        out_shape=jax.ShapeDtypeStruct((b, t, A_W), F32),
        scratch_shapes=[pltpu.VMEM((b * (A_W // gw), gw, gw), F32)],
        compiler_params=_cparams(1),
        name="wkv",
    )(r, k, v, na, nb, lw, cs, gate, ln_w.reshape(1, A_W), ln_b.reshape(1, A_W),
      r_k.reshape(1, A_W), ones_bd)


def _rope_angles_kernel(pos_ref, freq_ref, cos_ref, sin_ref, nsin_ref):
    ang = pos_ref[...].astype(F32) * freq_ref[...]
    sin = jnp.sin(ang)
    cos_ref[...] = jnp.cos(ang)
    sin_ref[...] = sin
    nsin_ref[...] = -sin


def _rope_tables(positions):
    b, t = positions.shape
    half = ROPE_DIM // 2
    n_rows = b * t * half // LANES
    inv_freq = jnp.power(jnp.float32(ROPE_THETA), -jnp.arange(half, dtype=F32) / half)
    pos_rep = jnp.repeat(positions.reshape(-1), half).reshape(n_rows, LANES)
    freq = jnp.tile(inv_freq, LANES // half).reshape(1, LANES)
    whole = pl.BlockSpec((n_rows, LANES), lambda i: (0, 0))
    cos_c, sin_c, nsin_c = pl.pallas_call(
        _rope_angles_kernel,
        grid=(1,),
        in_specs=[whole, pl.BlockSpec((1, LANES), lambda i: (0, 0))],
        out_specs=[whole] * 3,
        out_shape=[jax.ShapeDtypeStruct((n_rows, LANES), F32)] * 3,
        compiler_params=_cparams(1),
        name="rope_angles",
    )(pos_rep, freq)
    cos8, sin8, nsin8 = (z.reshape(b, t, half) for z in (cos_c, sin_c, nsin_c))
    rest = HEAD - ROPE_DIM
    cos_head = jnp.concatenate([cos8, cos8, jnp.ones((b, t, rest), F32)], axis=-1)
    sin_head = jnp.concatenate([nsin8, sin8, jnp.zeros((b, t, rest), F32)], axis=-1)
    n_heads = LANES // HEAD
    return jnp.tile(cos_head, (1, 1, n_heads)), jnp.tile(sin_head, (1, 1, n_heads))


def _rope(x, cos_t, sin_t, first8):
    fwd = pltpu.roll(x, LANES - ROPE_DIM // 2, axis=1)
    bwd = pltpu.roll(x, ROPE_DIM // 2, axis=1)
    return x * cos_t + jnp.where(first8, fwd, bwd) * sin_t


def _dup_head(tile, which, m0):
    swapped = pltpu.roll(tile, HEAD, axis=1)
    if which == 0:
        return jnp.where(m0, tile, swapped)
    return jnp.where(m0, swapped, tile)


def _lane_tile(j):
    return slice(j * LANES, (j + 1) * LANES)


def _swa_consts():
    blk = WINDOW
    m0, m1 = _head_masks()
    lane = lax.broadcasted_iota(jnp.int32, (1, LANES), 1)
    first8 = (lane & (HEAD - 1)) < ROPE_DIM // 2
    qi = lax.broadcasted_iota(jnp.int32, (blk, 2 * blk), 0) + blk
    ki = lax.broadcasted_iota(jnp.int32, (blk, 2 * blk), 1)
    band = (ki <= qi) & (qi - ki < WINDOW)
    in_cur = ki >= blk
    ri = lax.broadcasted_iota(jnp.int32, (4 * blk, LANES), 0)
    ci = lax.broadcasted_iota(jnp.int32, (4 * blk, LANES), 1)
    sum_cols = jnp.where((ri >= 2 * blk) == (ci >= HEAD), 1.0, 0.0).astype(BF16)
    return m0, m1, first8, band, in_cur, sum_cols


def _swa_scores(q_tile_at, k_win, v_win, cos_c, sin_c, consts):
    m0, m1, first8, _, _, sum_cols = consts
    scale = 1.0 / math.sqrt(HEAD)
    k2s, rhs_pv = [], []
    for kt in range(B_KVW // LANES):
        for which in range(2):
            k2s.append(_dup_head(k_win[kt], which, m0).astype(BF16))
            v2 = _dup_head(v_win[kt], which, m0)
            vstack = jnp.concatenate([jnp.where(m0, v2, 0.0), jnp.where(m1, v2, 0.0)], axis=0)
            rhs_pv.append(jnp.concatenate([vstack.astype(BF16), sum_cols], axis=1))
    scores = []
    for g in range(B_KVW // HEAD):
        rows = []
        for pr in range(B_GROUP // 2):
            qp = _rope(q_tile_at(2 * g + pr), cos_c, sin_c, first8) * scale
            rows += [jnp.where(m0, qp, 0.0), jnp.where(m1, qp, 0.0)]
        scores.append(_bdot_nt(jnp.concatenate(rows, axis=0), k2s[g]))
    return scores, rhs_pv


def _swa_outputs(scores, rhs_pv, has_prev, sink_ref, gate_tile_at, store_tile, consts):
    m0, _, _, band, in_cur, _ = consts
    blk = WINDOW
    valid = band & (has_prev | in_cur)
    for g in range(B_KVW // HEAD):
        for pr in range(B_GROUP // 2):
            qt = 2 * g + pr
            probs, sink_terms = [], []
            for half in range(2):
                r0 = (2 * pr + half) * blk
                s = jnp.where(valid, scores[g][r0:r0 + blk], -jnp.inf)
                sink = sink_ref[:, qt * LANES + half * HEAD:qt * LANES + half * HEAD + 1]
                m = jnp.maximum(jnp.max(s, axis=-1, keepdims=True), sink)
                probs.append(jnp.exp(s - m).astype(BF16))
                sink_terms.append(jnp.exp(sink - m))
            od = jnp.dot(jnp.concatenate(probs, axis=1), rhs_pv[g], preferred_element_type=F32)
            den = od[:, LANES:] + jnp.where(m0, sink_terms[0], sink_terms[1])
            store_tile(qt, od[:, :LANES] * (1.0 / den) * _silu(gate_tile_at(qt)))


def _out_proj_kernel(x_ref, ya_ref, yb_ref, w_ref, o_ref):
    y = jnp.concatenate([ya_ref[...].astype(BF16), yb_ref[...].astype(BF16)], axis=1)
    o_ref[...] = x_ref[...] + jnp.dot(y, w_ref[...], preferred_element_type=F32)


def _out_proj(x2d, ya, yb, w_bf16, tm):
    n, d = x2d.shape
    return pl.pallas_call(
        _out_proj_kernel,
        grid=(n // tm,),
        in_specs=[
            pl.BlockSpec((tm, d), lambda i: (i, 0)),
            pl.BlockSpec((tm, A_W), lambda i: (i, 0)),
            pl.BlockSpec((tm, B_W), lambda i: (i, 0)),
            pl.BlockSpec((A_W + B_W, d), lambda i: (0, 0), pipeline_mode=pl.Buffered(1)),
        ],
        out_specs=pl.BlockSpec((tm, d), lambda i: (i, 0)),
        out_shape=jax.ShapeDtypeStruct((n, d), F32),
        compiler_params=_cparams(1),
        name="out_proj",
    )(x2d, ya, yb, w_bf16)


def _odd_layer_kernel(x_ref, g_ref, win_ref, lnw_ref, lnb_ref, ws_ref, bst_ref, wout_ref,
                      fin_ref, o_ref, y_ref, *, final_norm):
    tm, d = x_ref.shape
    ch = SGU_CHUNK
    x = x_ref[...]
    ms = jnp.mean(x * x, axis=-1, keepdims=True)
    h = ((x * lax.rsqrt(ms + RMS_EPS)) * g_ref[...]).astype(BF16)
    v = jnp.dot(h, win_ref[:, d:2 * d], preferred_element_type=F32)
    mean = jnp.mean(v, axis=-1, keepdims=True)
    dv = v - mean
    var = jnp.mean(dv * dv, axis=-1, keepdims=True)
    vn = ((dv * lax.rsqrt(var + LN_EPS)) * lnw_ref[...] + lnb_ref[...]).astype(BF16)
    ti = lax.broadcasted_iota(jnp.int32, (ch, ch), 0)
    si = lax.broadcasted_iota(jnp.int32, (ch, ch), 1)
    causal = ti >= si
    cw = 2 * LANES
    zero_tile = jnp.zeros((ch, LANES), BF16)
    for j in range(d // cw):
        c0 = j * cw
        u = jnp.dot(h, win_ref[:, c0:c0 + cw], preferred_element_type=F32)
        gate = jnp.dot(h, win_ref[:, 2 * d + c0:2 * d + c0 + cw], preferred_element_type=F32)
        wm = jnp.concatenate(
            [jnp.where(causal, ws_ref[2 * j + gi], 0.0).astype(BF16) for gi in range(2)], axis=1)
        bias = jnp.concatenate(
            [jnp.broadcast_to(bst_ref[:, 2 * j + gi:2 * j + gi + 1], (ch, LANES)) for gi in range(2)],
            axis=1)
        mixed = []
        for ci in range(tm // ch):
            vc = vn[ci * ch:(ci + 1) * ch, c0:c0 + cw]
            v_bd = jnp.concatenate(
                [jnp.concatenate([vc[:, :LANES], zero_tile], axis=1),
                 jnp.concatenate([zero_tile, vc[:, LANES:]], axis=1)], axis=0)
            mixed.append(jnp.dot(wm, v_bd, preferred_element_type=F32) + bias)
        y_ref[:, c0:c0 + cw] = (u * jnp.concatenate(mixed, axis=0) * _silu(gate)).astype(BF16)
    out = x + jnp.dot(y_ref[...], wout_ref[...], preferred_element_type=F32)
    if final_norm:
        ms = jnp.mean(out * out, axis=-1, keepdims=True)
        out = (out * lax.rsqrt(ms + RMS_EPS)) * fin_ref[...]
    o_ref[...] = out


def _odd_layer(x2d, g, w_in_bf16, ln_w, ln_b, ws, bs, w_out_bf16, final_g, tm):
    n, d = x2d.shape
    final_norm = final_g is not None
    fin = final_g if final_norm else jnp.ones((d,), F32)
    row_tile = pl.BlockSpec((tm, d), lambda i: (i, 0))
    vec = pl.BlockSpec((1, d), lambda i: (0, 0))
    return pl.pallas_call(
        functools.partial(_odd_layer_kernel, final_norm=final_norm),
        grid=(n // tm,),
        in_specs=[
            row_tile, vec,
            pl.BlockSpec((d, 3 * d), lambda i: (0, 0), pipeline_mode=pl.Buffered(1)),
            vec, vec,
            pl.BlockSpec((SGU_GROUPS, SGU_CHUNK, SGU_CHUNK), lambda i: (0, 0, 0)),
            pl.BlockSpec((SGU_CHUNK, SGU_GROUPS), lambda i: (0, 0)),
            pl.BlockSpec((d, d), lambda i: (0, 0), pipeline_mode=pl.Buffered(1)),
            vec,
        ],
        out_specs=row_tile,
        out_shape=jax.ShapeDtypeStruct((n, d), F32),
        scratch_shapes=[pltpu.VMEM((tm, d), BF16)],
        compiler_params=_cparams(1),
        name="odd_layer",
    )(x2d, g.reshape(1, d), w_in_bf16, ln_w.reshape(1, d), ln_b.reshape(1, d), ws, bs.T,
      w_out_bf16, fin.reshape(1, d))


def kernel(x, positions, e_norm, e_w_in, e_mu, rwkv_w0, rwkv_w2, rwkv_a0, rwkv_a2, rwkv_k_k, rwkv_k_a, rwkv_r_k, rwkv_ln_w, rwkv_ln_b, rwkv_v0, rwkv_v1, rwkv_v2, attn_sinks, e_w_out, o_norm, o_w_in, sgu_ln_w, sgu_ln_b, sgu_ws, sgu_bs, o_w_out, final_norm):
    b, t, d = x.shape
    n = b * t
    depth = e_norm.shape[0] + o_norm.shape[0]
    assert t % 256 == 0 and d == D_MODEL and depth % 2 == 0
    x2d = x.reshape(n, d)
    cos_t, sin_t = _rope_tables(positions)
    v_first = None
    for layer in range(depth):
        if layer % 2 == 0:
            e = layer // 2
            w_in = e_w_in[e].astype(BF16)
            r3 = lambda z: z.reshape(b, t, z.shape[-1])
            vres = None if e == 0 else (v_first, rwkv_v0[e - 1], rwkv_v1[e - 1], rwkv_v2[e - 1])
            a_gate, q, kv, b_gate = _norm_matmul(
                x2d, e_norm[e], w_in[:, SHIFT_W:], (A_W, B_W, 2 * B_KVW, B_W), tm=512)
            r, k2, v, na, nb, lw, cs, yb = _prep_swa(
                r3(x2d), e_norm[e], w_in[:, :SHIFT_W], e_mu[e], rwkv_w0[e], rwkv_w2[e], rwkv_a0[e],
                rwkv_a2[e], rwkv_k_k[e], rwkv_k_a[e], vres, r3(q), r3(kv), cos_t, sin_t, r3(b_gate),
                attn_sinks[e], tt=256)
            if e == 0:
                v_first = v
            ya = _wkv(r, k2, v, na, nb, lw, cs, r3(a_gate), rwkv_ln_w[e], rwkv_ln_b[e], rwkv_r_k[e])
            x2d = _out_proj(x2d, ya.reshape(n, A_W), yb.reshape(n, B_W), e_w_out[e].astype(BF16), tm=512)
        else:
            o = layer // 2
            fin = final_norm if layer == depth - 1 else None
            x2d = _odd_layer(x2d, o_norm[o], o_w_in[o].astype(BF16), sgu_ln_w[o], sgu_ln_b[o],
                             sgu_ws[o], sgu_bs[o], o_w_out[o].astype(BF16), fin, tm=256)
    return x2d.reshape(b, t, d)
```

```python
import functools
import math

import jax
import jax.numpy as jnp
from jax import lax
from jax.experimental import pallas as pl
from jax.experimental.pallas import tpu as pltpu

F32 = jnp.float32
BF16 = jnp.bfloat16

D_MODEL = 2048
HEAD = 64
A_W = 1024
B_W = 1024
B_KVW = 256
B_GROUP = 4
LORA = 64
SHIFT_W = 3 * A_W + 2 * LORA
WINDOW = 128
ROPE_DIM = 16
ROPE_THETA = 500000.0
SGU_CHUNK = 128
SGU_GROUPS = 16
RMS_EPS = 1e-5
LN_EPS = 1e-5
LN_X_EPS = HEAD * 1e-5

LANES = 128
WKV_CHUNK = 64
WKV_GROUP = 4
WKV_STEP_CHUNKS = 2
VMEM_LIMIT = 56 * 1024 * 1024


def _cparams(n_axes):
    return pltpu.CompilerParams(
        dimension_semantics=("arbitrary",) * n_axes, vmem_limit_bytes=VMEM_LIMIT)


def _bdot(a, b):
    return jnp.dot(a.astype(BF16), b.astype(BF16), preferred_element_type=F32)


def _bdot_nt(a, b):
    return lax.dot_general(a.astype(BF16), b.astype(BF16), (((1,), (1,)), ((), ())),
                           preferred_element_type=F32)


def _bdot_tn(a, b):
    return lax.dot_general(a.astype(BF16), b.astype(BF16), (((0,), (0,)), ((), ())),
                           preferred_element_type=F32)


def _split_dot(x, ones_bf16, passes):
    acc = None
    rem = x
    for _ in range(passes):
        piece = rem.astype(BF16)
        term = jnp.dot(piece, ones_bf16, preferred_element_type=F32)
        acc = term if acc is None else acc + term
        rem = rem - piece.astype(F32)
    return acc


def _silu(x):
    return x * (1.0 / (1.0 + jnp.exp(-x)))


def _sigmoid(x):
    return 1.0 / (1.0 + jnp.exp(-x))


def _head_masks():
    lane = lax.broadcasted_iota(jnp.int32, (1, LANES), 1)
    return lane < HEAD, lane >= HEAD


def _norm_matmul_kernel(x_ref, g_ref, w_ref, *out_refs, seg_widths, col_chunk):
    x = x_ref[...]
    ms = jnp.mean(x * x, axis=-1, keepdims=True)
    h = ((x * lax.rsqrt(ms + RMS_EPS)) * g_ref[...]).astype(BF16)
    col = 0
    for o_ref, width in zip(out_refs, seg_widths):
        for c0 in range(0, width, col_chunk):
            cw = min(col_chunk, width - c0)
            o_ref[:, c0:c0 + cw] = jnp.dot(
                h, w_ref[:, col + c0:col + c0 + cw], preferred_element_type=F32)
        col += width


def _norm_matmul(x2d, g, w_bf16, seg_widths, tm):
    n, d = x2d.shape
    ncol = w_bf16.shape[1]
    assert sum(seg_widths) == ncol and n % tm == 0
    kern = functools.partial(_norm_matmul_kernel, seg_widths=tuple(seg_widths), col_chunk=512)
    return pl.pallas_call(
        kern,
        grid=(n // tm,),
        in_specs=[
            pl.BlockSpec((tm, d), lambda i: (i, 0)),
            pl.BlockSpec((1, d), lambda i: (0, 0)),
            pl.BlockSpec((d, ncol), lambda i: (0, 0), pipeline_mode=pl.Buffered(1)),
        ],
        out_specs=[pl.BlockSpec((tm, w), lambda i: (i, 0)) for w in seg_widths],
        out_shape=[jax.ShapeDtypeStruct((n, w), F32) for w in seg_widths],
        compiler_params=_cparams(1),
        name="norm_matmul",
    )(x2d, g.reshape(1, d), w_bf16)


def _prep_swa_kernel(*refs, has_vres):
    (x_ref, g_ref, w_ref, mu_ref, w0_ref, a0_ref, wa2_ref, kk_ref, ka_ref, tri_ref, ones_ref) = refs[:11]
    if has_vres:
        vf_ref, v0_ref, v1_ref, v2_ref = refs[11:15]
    (q_ref, kvc_ref, kvp_ref, cosc_ref, sinc_ref, cosp_ref, sinp_ref, bg_ref, sink_ref,
     r_out, k_out, v_out, na_out, nb_out, lw_out, cs_out, yb_out, carry_ref) = refs[-18:]

    @pl.when(pl.program_id(1) == 0)
    def _():
        carry_ref[...] = jnp.zeros_like(carry_ref)

    x = x_ref[0]
    tt = x.shape[0]

    consts = _swa_consts()
    first8 = consts[2]
    blk = WINDOW
    n_blk = tt // blk
    k_roped, v_rows = [], []
    for j in range(n_blk + 1):
        if j == 0:
            kv, cos_j, sin_j = kvp_ref[0], cosp_ref[0], sinp_ref[0]
        else:
            rows = slice((j - 1) * blk, j * blk)
            kv, cos_j, sin_j = kvc_ref[0, rows], cosc_ref[0, rows], sinc_ref[0, rows]
        k_roped.append([_rope(kv[:, _lane_tile(kt)], cos_j, sin_j, first8)
                        for kt in range(B_KVW // LANES)])
        v_rows.append([kv[:, B_KVW + kt * LANES:B_KVW + (kt + 1) * LANES]
                       for kt in range(B_KVW // LANES)])
    attn = []
    for j in range(n_blk):
        rows = slice(j * blk, (j + 1) * blk)
        k_win = [jnp.concatenate([k_roped[j][kt], k_roped[j + 1][kt]], axis=0)
                 for kt in range(B_KVW // LANES)]
        v_win = [jnp.concatenate([v_rows[j][kt], v_rows[j + 1][kt]], axis=0)
                 for kt in range(B_KVW // LANES)]
        attn.append(_swa_scores(lambda qt, rows=rows: q_ref[0, rows, _lane_tile(qt)], k_win, v_win,
                                cosc_ref[0, rows], sinc_ref[0, rows], consts))

    ms = jnp.mean(x * x, axis=-1, keepdims=True)
    h = ((x * lax.rsqrt(ms + RMS_EPS)) * g_ref[...]).astype(BF16)
    row = lax.broadcasted_iota(jnp.int32, (tt, 1), 0)

    def shifted(c0, width):
        z = jnp.dot(h, w_ref[:, c0:c0 + width], preferred_element_type=F32)
        prev = jnp.where(row == 0, carry_ref[0:1, c0:c0 + width], pltpu.roll(z, 1, axis=0))
        carry_ref[0:1, c0:c0 + width] = z[tt - 1:tt, :]
        return z + (prev - z) * mu_ref[:, c0:c0 + width]

    xwa = shifted(3 * A_W, 2 * LORA)
    lane = lax.broadcasted_iota(jnp.int32, (1, 2 * LORA), 1)
    lora_in = jnp.where(lane < LORA, jnp.tanh(xwa), xwa)
    lora = _bdot(lora_in, wa2_ref[...])
    wpre = w0_ref[...] + lora[:, 0:A_W]
    neg = -wpre
    softplus = jnp.maximum(neg, 0.0) + jnp.log(1.0 + jnp.exp(-jnp.abs(neg)))
    w_log = -softplus - 0.5
    lw = -jnp.exp(w_log)
    a = _sigmoid(a0_ref[...] + lora[:, A_W:2 * A_W])
    lw_out[0] = lw
    r_out[0] = shifted(0, A_W)
    hi = lw.astype(BF16)
    lo = (lw - hi.astype(F32)).astype(BF16)
    tri = tri_ref[...]
    cs_out[0] = (jnp.dot(tri, hi, preferred_element_type=F32)
                 + jnp.dot(tri, lo, preferred_element_type=F32))
    k = shifted(A_W, A_W)
    k_out[0] = k * (1.0 + (a - 1.0) * ka_ref[...])
    kk = k * kk_ref[...]
    ones_bd = ones_ref[...]
    gw = ones_bd.shape[0]
    for p in range(A_W // gw):
        sl = slice(p * gw, (p + 1) * gw)
        kkp = kk[:, sl]
        ss = _bdot(kkp * kkp, ones_bd)
        kkn = kkp * lax.rsqrt(ss + 1e-12)
        na_out[0, :, sl] = -kkn
        nb_out[0, :, sl] = kkn * a[:, sl]
    v = shifted(2 * A_W, A_W)
    if has_vres:
        vf = vf_ref[0]
        gate = _sigmoid(v0_ref[...] + _bdot(_bdot(v, v1_ref[...]), v2_ref[...]))
        v = v + (vf - v) * gate
    v_out[0] = v

    for j in range(n_blk):
        rows = slice(j * blk, (j + 1) * blk)
        has_prev = (pl.program_id(1) > 0) if j == 0 else True

        def store_tile(qt, val, rows=rows):
            yb_out[0, rows, _lane_tile(qt)] = val

        _swa_outputs(attn[j][0], attn[j][1], has_prev, sink_ref,
                     lambda qt, rows=rows: bg_ref[0, rows, _lane_tile(qt)], store_tile, consts)


def _prep_swa(x3d, g, w_a, mu, w0, w2, a0, a2, k_k, k_a, vres, q, kv, cos_t, sin_t, b_gate, sinks, tt):
    b, t, d = x3d.shape
    has_vres = vres is not None
    wa2 = jnp.zeros((2 * LORA, 2 * A_W), F32)
    wa2 = wa2.at[:LORA, :A_W].set(w2).at[LORA:, A_W:].set(a2).astype(BF16)
    ti = jnp.arange(tt)
    tri = ((ti[:, None] >= ti[None, :]) &
           (ti[:, None] // WKV_CHUNK == ti[None, :] // WKV_CHUNK)).astype(BF16)
    gw = WKV_GROUP * HEAD
    li = jnp.arange(gw)
    ones_bd = (li[:, None] // HEAD == li[None, :] // HEAD).astype(BF16)
    row = lambda p: p.reshape(1, -1)
    full = lambda shape: pl.BlockSpec(shape, lambda bi, i: (0,) * len(shape))
    tile = lambda w: pl.BlockSpec((1, tt, w), lambda bi, i: (bi, i, 0))
    in_specs = [
        tile(d), full((1, d)),
        pl.BlockSpec((d, SHIFT_W), lambda bi, i: (0, 0), pipeline_mode=pl.Buffered(1)),
        full((1, SHIFT_W)), full((1, A_W)), full((1, A_W)), full((2 * LORA, 2 * A_W)),
        full((1, A_W)), full((1, A_W)), full((tt, tt)), full((gw, gw)),
    ]
    args = [x3d, row(g), w_a, row(mu), row(w0), row(a0), wa2, row(k_k), row(k_a), tri, ones_bd]
    if has_vres:
        v_first, v0, v1, v2 = vres
        in_specs += [tile(A_W), full((1, A_W)), full(v1.shape), full(v2.shape)]
        args += [v_first, row(v0), v1.astype(BF16), v2.astype(BF16)]
    blk = WINDOW
    prev_blk = lambda w: pl.BlockSpec(
        (1, blk, w), lambda bi, i: (bi, jnp.maximum(i * (tt // blk) - 1, 0), 0))
    in_specs += [tile(B_W), tile(2 * B_KVW), prev_blk(2 * B_KVW), tile(LANES), tile(LANES),
                 prev_blk(LANES), prev_blk(LANES), tile(B_W), full((1, B_W))]
    args += [q, kv, kv, cos_t, sin_t, cos_t, sin_t, b_gate,
             jnp.repeat(sinks.astype(F32), HEAD).reshape(1, B_W)]
    return pl.pallas_call(
        functools.partial(_prep_swa_kernel, has_vres=has_vres),
        grid=(b, t // tt),
        in_specs=in_specs,
        out_specs=[tile(A_W)] * 7 + [tile(B_W)],
        out_shape=[jax.ShapeDtypeStruct((b, t, A_W), F32)] * 7 + [jax.ShapeDtypeStruct((b, t, B_W), F32)],
        scratch_shapes=[pltpu.VMEM((8, SHIFT_W), F32)],
        compiler_params=_cparams(2),
        name="prep_swa",
    )(*args)


def _wkv_kernel(r_ref, k_ref, v_ref, na_ref, nb_ref, lw_ref, cs_ref, g_ref,
                lnw_ref, lnb_ref, rk_ref, ones_ref, o_ref, s_ref):
    c = WKV_CHUNK

    @pl.when(pl.program_id(0) == 0)
    def _():
        s_ref[...] = jnp.zeros_like(s_ref)

    gw = WKV_GROUP * HEAD
    lane = lax.broadcasted_iota(jnp.int32, (1, gw), 1)
    head_masks = [(lane >= h * HEAD) & (lane < (h + 1) * HEAD) for h in range(WKV_GROUP)]
    ti = lax.broadcasted_iota(jnp.int32, (c, gw), 0)
    tj = lax.broadcasted_iota(jnp.int32, (c, gw), 1) & (c - 1)
    strict = ti > tj
    incl = ti >= tj
    eye_t = jnp.where(ti == tj, 1.0, 0.0)
    li = lax.broadcasted_iota(jnp.int32, (gw, gw), 0)
    lj = lax.broadcasted_iota(jnp.int32, (gw, gw), 1)
    bdmask = (li // HEAD) == (lj // HEAD)
    ones_bd = ones_ref[...]

    def bd(x):
        xb = x.astype(BF16)
        zero = jnp.zeros_like(xb)
        return jnp.concatenate([jnp.where(m, xb, zero) for m in head_masks], axis=0)

    n_ck = r_ref.shape[1] // c
    n_state = s_ref.shape[0]
    chains = [(bi, slice(ck * c, (ck + 1) * c), slice(p * gw, (p + 1) * gw))
              for ck in range(n_ck) for bi in range(r_ref.shape[0]) for p in range(A_W // gw)]
    pairs = range(len(chains))
    rd = lambda ref, p: ref[chains[p]]
    rt, at_abs, bh, kh, wc, em, lhs0, rhs0 = [], [], [], [], [], [], [], []
    for p in pairs:
        cs = rd(cs_ref, p)
        mid = cs[c // 2 - 1:c // 2, :]
        last = cs[c - 1:c, :]
        e_neg = jnp.exp(mid - cs)
        rt_p = rd(r_ref, p) * jnp.exp(cs - mid)
        at_p = rd(na_ref, p) * jnp.exp(cs - rd(lw_ref, p) - mid)
        kt_p = rd(k_ref, p) * e_neg
        bt_p = rd(nb_ref, p) * e_neg
        em_p = jnp.exp(mid)
        e_end = jnp.exp(last - mid)
        rt.append(rt_p)
        at_abs.append(at_p * em_p)
        em.append(em_p)
        wc.append(jnp.exp(last))
        bh.append(bt_p * e_end)
        kh.append(kt_p * e_end)
        lhs0.append(jnp.concatenate([at_p, rt_p], axis=0))
        rhs0.append(jnp.concatenate([bd(bt_p), bd(kt_p)], axis=0))
    sc = [_bdot_nt(lhs0[p], rhs0[p]) for p in pairs]
    a_ab = [jnp.where(strict, sc[p][:c, :gw], 0.0) for p in pairs]
    a_rb = [jnp.where(incl, sc[p][c:, :gw], 0.0) for p in pairs]
    akrk = [jnp.concatenate([jnp.where(strict, sc[p][:c, gw:], 0.0),
                             jnp.where(incl, sc[p][c:, gw:], 0.0)], axis=0) for p in pairs]
    xv = [_bdot(akrk[p], bd(rd(v_ref, p))) for p in pairs]
    pinv = [eye_t + a_ab[p] for p in pairs]
    apow = [_bdot(a_ab[p], bd(a_ab[p])) for p in pairs]
    for _ in range(4):
        both = [_bdot(jnp.concatenate([apow[p], pinv[p]], axis=0), bd(apow[p])) for p in pairs]
        pinv = [pinv[p] + both[p][c:] for p in pairs]
        apow = [both[p][:c] for p in pairs]
    pinv = [pinv[p] + _bdot(pinv[p], bd(apow[p])) for p in pairs]
    ta = [_bdot(pinv[p], jnp.concatenate([bd(xv[p][:c]), bd(at_abs[p])], axis=1)) for p in pairs]
    uv = [ta[p][:, :gw] for p in pairs]
    ap = [ta[p][:, gw:] for p in pairs]
    rb = [_bdot(a_rb[p], jnp.concatenate([bd(ap[p]), bd(uv[p])], axis=1)) for p in pairs]
    rpap = [jnp.concatenate([rt[p] * em[p] + rb[p][:, :gw], ap[p]], axis=0) for p in pairs]
    yv = [rb[p][:, gw:] + xv[p][c:] for p in pairs]
    bk = [jnp.concatenate([bh[p], kh[p]], axis=0) for p in pairs]
    state = [s_ref[si] for si in range(n_state)]
    y = []
    for ck in range(n_ck):
        ps = [ck * n_state + si for si in range(n_state)]
        ru = [_bdot_nt(rpap[p], state[si]) for si, p in enumerate(ps)]
        y += [ru[si][:c] + yv[p] for si, p in enumerate(ps)]
        grams = [_bdot_tn(jnp.concatenate([ru[si][c:] + uv[p], rd(v_ref, p)], axis=0), bk[p])
                 for si, p in enumerate(ps)]
        state = [state[si] * wc[p] + jnp.where(bdmask, grams[si], 0.0) for si, p in enumerate(ps)]
    for si in range(n_state):
        s_ref[si] = state[si]
    stat_rows = []
    for p in pairs:
        stat_rows += [y[p], rd(r_ref, p) * rd(k_ref, p) * rk_ref[:, chains[p][2]]]
    stats_all = _bdot(jnp.concatenate(stat_rows, axis=0), ones_bd)
    stats = [stats_all[2 * c * p:2 * c * (p + 1)] for p in pairs]
    d = [y[p] - stats[p][:c] * (1.0 / HEAD) for p in pairs]
    var_all = _bdot(jnp.concatenate([d[p] * d[p] for p in pairs], axis=0), ones_bd) * (1.0 / HEAD)
    var = [var_all[c * p:c * (p + 1)] for p in pairs]
    for p in pairs:
        sl = chains[p][2]
        yn = d[p] * lax.rsqrt(var[p] + LN_X_EPS) * lnw_ref[:, sl] + lnb_ref[:, sl]
        out = yn + stats[p][c:] * rd(v_ref, p)
        o_ref[chains[p]] = out * _silu(rd(g_ref, p))


def _wkv(r, k, v, na, nb, lw, cs, gate, ln_w, ln_b, r_k):
    b, t, _ = r.shape
    c = WKV_CHUNK
    gw = WKV_GROUP * HEAD
    li = jnp.arange(gw)
    ones_bd = (li[:, None] // HEAD == li[None, :] // HEAD).astype(BF16)
    rows = WKV_STEP_CHUNKS * c
    assert t % rows == 0
    tile = pl.BlockSpec((b, rows, A_W), lambda i: (0, i, 0))
    vec = pl.BlockSpec((1, A_W), lambda i: (0, 0))
    return pl.pallas_call(
        _wkv_kernel,
        grid=(t // rows,),
        in_specs=[tile] * 8 + [vec] * 3 + [pl.BlockSpec((gw, gw), lambda i: (0, 0))],
        out_specs=tile,
        out_shape=jax.ShapeDtypeStruct((b, t, A_W), F32),
        scratch_shapes=[pltpu.VMEM((b * (A_W // gw), gw, gw), F32)],
        compiler_params=_cparams(1),
        name="wkv",
    )(r, k, v, na, nb, lw, cs, gate, ln_w.reshape(1, A_W), ln_b.reshape(1, A_W),
      r_k.reshape(1, A_W), ones_bd)


def _rope_angles_kernel(pos_ref, freq_ref, cos_ref, sin_ref, nsin_ref):
    ang = pos_ref[...].astype(F32) * freq_ref[...]
    sin = jnp.sin(ang)
    cos_ref[...] = jnp.cos(ang)
    sin_ref[...] = sin
    nsin_ref[...] = -sin


def _rope_tables(positions):
    b, t = positions.shape
    half = ROPE_DIM // 2
    n_rows = b * t * half // LANES
    inv_freq = jnp.power(jnp.float32(ROPE_THETA), -jnp.arange(half, dtype=F32) / half)
    pos_rep = jnp.repeat(positions.reshape(-1), half).reshape(n_rows, LANES)
    freq = jnp.tile(inv_freq, LANES // half).reshape(1, LANES)
    whole = pl.BlockSpec((n_rows, LANES), lambda i: (0, 0))
    cos_c, sin_c, nsin_c = pl.pallas_call(
        _rope_angles_kernel,
        grid=(1,),
        in_specs=[whole, pl.BlockSpec((1, LANES), lambda i: (0, 0))],
        out_specs=[whole] * 3,
        out_shape=[jax.ShapeDtypeStruct((n_rows, LANES), F32)] * 3,
        compiler_params=_cparams(1),
        name="rope_angles",
    )(pos_rep, freq)
    cos8, sin8, nsin8 = (z.reshape(b, t, half) for z in (cos_c, sin_c, nsin_c))
    rest = HEAD - ROPE_DIM
    cos_head = jnp.concatenate([cos8, cos8, jnp.ones((b, t, rest), F32)], axis=-1)
    sin_head = jnp.concatenate([nsin8, sin8, jnp.zeros((b, t, rest), F32)], axis=-1)
    n_heads = LANES // HEAD
    return jnp.tile(cos_head, (1, 1, n_heads)), jnp.tile(sin_head, (1, 1, n_heads))


def _rope(x, cos_t, sin_t, first8):
    fwd = pltpu.roll(x, LANES - ROPE_DIM // 2, axis=1)
    bwd = pltpu.roll(x, ROPE_DIM // 2, axis=1)
    return x * cos_t + jnp.where(first8, fwd, bwd) * sin_t


def _dup_head(tile, which, m0):
    swapped = pltpu.roll(tile, HEAD, axis=1)
    if which == 0:
        return jnp.where(m0, tile, swapped)
    return jnp.where(m0, swapped, tile)


def _lane_tile(j):
    return slice(j * LANES, (j + 1) * LANES)


def _swa_consts():
    blk = WINDOW
    m0, m1 = _head_masks()
    lane = lax.broadcasted_iota(jnp.int32, (1, LANES), 1)
    first8 = (lane & (HEAD - 1)) < ROPE_DIM // 2
    qi = lax.broadcasted_iota(jnp.int32, (blk, 2 * blk), 0) + blk
    ki = lax.broadcasted_iota(jnp.int32, (blk, 2 * blk), 1)
    band = (ki <= qi) & (qi - ki < WINDOW)
    in_cur = ki >= blk
    ri = lax.broadcasted_iota(jnp.int32, (4 * blk, LANES), 0)
    ci = lax.broadcasted_iota(jnp.int32, (4 * blk, LANES), 1)
    sum_cols = jnp.where((ri >= 2 * blk) == (ci >= HEAD), 1.0, 0.0).astype(BF16)
    return m0, m1, first8, band, in_cur, sum_cols


def _swa_scores(q_tile_at, k_win, v_win, cos_c, sin_c, consts):
    m0, m1, first8, _, _, sum_cols = consts
    scale = 1.0 / math.sqrt(HEAD)
    k2s, rhs_pv = [], []
    for kt in range(B_KVW // LANES):
        for which in range(2):
            k2s.append(_dup_head(k_win[kt], which, m0).astype(BF16))
            v2 = _dup_head(v_win[kt], which, m0)
            vstack = jnp.concatenate([jnp.where(m0, v2, 0.0), jnp.where(m1, v2, 0.0)], axis=0)
            rhs_pv.append(jnp.concatenate([vstack.astype(BF16), sum_cols], axis=1))
    scores = []
    for g in range(B_KVW // HEAD):
        rows = []
        for pr in range(B_GROUP // 2):
            qp = _rope(q_tile_at(2 * g + pr), cos_c, sin_c, first8) * scale
            rows += [jnp.where(m0, qp, 0.0), jnp.where(m1, qp, 0.0)]
        scores.append(_bdot_nt(jnp.concatenate(rows, axis=0), k2s[g]))
    return scores, rhs_pv


def _swa_outputs(scores, rhs_pv, has_prev, sink_ref, gate_tile_at, store_tile, consts):
    m0, _, _, band, in_cur, _ = consts
    blk = WINDOW
    valid = band & (has_prev | in_cur)
    for g in range(B_KVW // HEAD):
        for pr in range(B_GROUP // 2):
            qt = 2 * g + pr
            probs, sink_terms = [], []
            for half in range(2):
                r0 = (2 * pr + half) * blk
                s = jnp.where(valid, scores[g][r0:r0 + blk], -jnp.inf)
                sink = sink_ref[:, qt * LANES + half * HEAD:qt * LANES + half * HEAD + 1]
                m = jnp.maximum(jnp.max(s, axis=-1, keepdims=True), sink)
                probs.append(jnp.exp(s - m).astype(BF16))
                sink_terms.append(jnp.exp(sink - m))
            od = jnp.dot(jnp.concatenate(probs, axis=1), rhs_pv[g], preferred_element_type=F32)
            den = od[:, LANES:] + jnp.where(m0, sink_terms[0], sink_terms[1])
            store_tile(qt, od[:, :LANES] * (1.0 / den) * _silu(gate_tile_at(qt)))


def _out_proj_kernel(x_ref, ya_ref, yb_ref, w_ref, o_ref):
    y = jnp.concatenate([ya_ref[...].astype(BF16), yb_ref[...].astype(BF16)], axis=1)
    o_ref[...] = x_ref[...] + jnp.dot(y, w_ref[...], preferred_element_type=F32)


def _out_proj(x2d, ya, yb, w_bf16, tm):
    n, d = x2d.shape
    return pl.pallas_call(
        _out_proj_kernel,
        grid=(n // tm,),
        in_specs=[
            pl.BlockSpec((tm, d), lambda i: (i, 0)),
            pl.BlockSpec((tm, A_W), lambda i: (i, 0)),
            pl.BlockSpec((tm, B_W), lambda i: (i, 0)),
            pl.BlockSpec((A_W + B_W, d), lambda i: (0, 0), pipeline_mode=pl.Buffered(1)),
        ],
        out_specs=pl.BlockSpec((tm, d), lambda i: (i, 0)),
        out_shape=jax.ShapeDtypeStruct((n, d), F32),
        compiler_params=_cparams(1),
        name="out_proj",
    )(x2d, ya, yb, w_bf16)


def _odd_layer_kernel(x_ref, g_ref, win_ref, lnw_ref, lnb_ref, ws_ref, bst_ref, wout_ref,
                      fin_ref, o_ref, y_ref, *, final_norm):
    tm, d = x_ref.shape
    ch = SGU_CHUNK
    x = x_ref[...]
    ms = jnp.mean(x * x, axis=-1, keepdims=True)
    h = ((x * lax.rsqrt(ms + RMS_EPS)) * g_ref[...]).astype(BF16)
    v = jnp.dot(h, win_ref[:, d:2 * d], preferred_element_type=F32)
    mean = jnp.mean(v, axis=-1, keepdims=True)
    dv = v - mean
    var = jnp.mean(dv * dv, axis=-1, keepdims=True)
    vn = ((dv * lax.rsqrt(var + LN_EPS)) * lnw_ref[...] + lnb_ref[...]).astype(BF16)
    ti = lax.broadcasted_iota(jnp.int32, (ch, ch), 0)
    si = lax.broadcasted_iota(jnp.int32, (ch, ch), 1)
    causal = ti >= si
    cw = 2 * LANES
    zero_tile = jnp.zeros((ch, LANES), BF16)
    for j in range(d // cw):
        c0 = j * cw
        u = jnp.dot(h, win_ref[:, c0:c0 + cw], preferred_element_type=F32)
        gate = jnp.dot(h, win_ref[:, 2 * d + c0:2 * d + c0 + cw], preferred_element_type=F32)
        wm = jnp.concatenate(
            [jnp.where(causal, ws_ref[2 * j + gi], 0.0).astype(BF16) for gi in range(2)], axis=1)
        bias = jnp.concatenate(
            [jnp.broadcast_to(bst_ref[:, 2 * j + gi:2 * j + gi + 1], (ch, LANES)) for gi in range(2)],
            axis=1)
        mixed = []
        for ci in range(tm // ch):
            vc = vn[ci * ch:(ci + 1) * ch, c0:c0 + cw]
            v_bd = jnp.concatenate(
                [jnp.concatenate([vc[:, :LANES], zero_tile], axis=1),
                 jnp.concatenate([zero_tile, vc[:, LANES:]], axis=1)], axis=0)
            mixed.append(jnp.dot(wm, v_bd, preferred_element_type=F32) + bias)
        y_ref[:, c0:c0 + cw] = (u * jnp.concatenate(mixed, axis=0) * _silu(gate)).astype(BF16)
    out = x + jnp.dot(y_ref[...], wout_ref[...], preferred_element_type=F32)
    if final_norm:
        ms = jnp.mean(out * out, axis=-1, keepdims=True)
        out = (out * lax.rsqrt(ms + RMS_EPS)) * fin_ref[...]
    o_ref[...] = out


def _odd_layer(x2d, g, w_in_bf16, ln_w, ln_b, ws, bs, w_out_bf16, final_g, tm):
    n, d = x2d.shape
    final_norm = final_g is not None
    fin = final_g if final_norm else jnp.ones((d,), F32)
    row_tile = pl.BlockSpec((tm, d), lambda i: (i, 0))
    vec = pl.BlockSpec((1, d), lambda i: (0, 0))
    return pl.pallas_call(
        functools.partial(_odd_layer_kernel, final_norm=final_norm),
        grid=(n // tm,),
        in_specs=[
            row_tile, vec,
            pl.BlockSpec((d, 3 * d), lambda i: (0, 0), pipeline_mode=pl.Buffered(1)),
            vec, vec,
            pl.BlockSpec((SGU_GROUPS, SGU_CHUNK, SGU_CHUNK), lambda i: (0, 0, 0)),
            pl.BlockSpec((SGU_CHUNK, SGU_GROUPS), lambda i: (0, 0)),
            pl.BlockSpec((d, d), lambda i: (0, 0), pipeline_mode=pl.Buffered(1)),
            vec,
        ],
        out_specs=row_tile,
        out_shape=jax.ShapeDtypeStruct((n, d), F32),
        scratch_shapes=[pltpu.VMEM((tm, d), BF16)],
        compiler_params=_cparams(1),
        name="odd_layer",
    )(x2d, g.reshape(1, d), w_in_bf16, ln_w.reshape(1, d), ln_b.reshape(1, d), ws, bs.T,
      w_out_bf16, fin.reshape(1, d))


def kernel(x, positions, e_norm, e_w_in, e_mu, rwkv_w0, rwkv_w2, rwkv_a0, rwkv_a2, rwkv_k_k, rwkv_k_a, rwkv_r_k, rwkv_ln_w, rwkv_ln_b, rwkv_v0, rwkv_v1, rwkv_v2, attn_sinks, e_w_out, o_norm, o_w_in, sgu_ln_w, sgu_ln_b, sgu_ws, sgu_bs, o_w_out, final_norm):
    b, t, d = x.shape
    n = b * t
    depth = e_norm.shape[0] + o_norm.shape[0]
    assert t % 256 == 0 and d == D_MODEL and depth % 2 == 0
    x2d = x.reshape(n, d)
    cos_t, sin_t = _rope_tables(positions)
    v_first = None
    for layer in range(depth):
        if layer % 2 == 0:
            e = layer // 2
            w_a = e_w_in[e, :, :SHIFT_W].astype(BF16)
            w_rest = e_w_in[e, :, SHIFT_W:].astype(BF16)
            r3 = lambda z: z.reshape(b, t, z.shape[-1])
            vres = None if e == 0 else (v_first, rwkv_v0[e - 1], rwkv_v1[e - 1], rwkv_v2[e - 1])
            a_gate, q, kv, b_gate = _norm_matmul(
                x2d, e_norm[e], w_rest, (A_W, B_W, 2 * B_KVW, B_W), tm=512)
            r, k2, v, na, nb, lw, cs, yb = _prep_swa(
                r3(x2d), e_norm[e], w_a, e_mu[e], rwkv_w0[e], rwkv_w2[e], rwkv_a0[e],
                rwkv_a2[e], rwkv_k_k[e], rwkv_k_a[e], vres, r3(q), r3(kv), cos_t, sin_t, r3(b_gate),
                attn_sinks[e], tt=256)
            if e == 0:
                v_first = v
            ya = _wkv(r, k2, v, na, nb, lw, cs, r3(a_gate), rwkv_ln_w[e], rwkv_ln_b[e], rwkv_r_k[e])
            x2d = _out_proj(x2d, ya.reshape(n, A_W), yb.reshape(n, B_W), e_w_out[e].astype(BF16), tm=512)
        else:
            o = layer // 2
            fin = final_norm if layer == depth - 1 else None
            x2d = _odd_layer(x2d, o_norm[o], o_w_in[o].astype(BF16), sgu_ln_w[o], sgu_ln_b[o],
                             sgu_ws[o], sgu_bs[o], o_w_out[o].astype(BF16), fin, tm=256)
    return x2d.reshape(b, t, d)
```

```python
import functools
import math

import jax
import jax.numpy as jnp
from jax import lax
from jax.experimental import pallas as pl
from jax.experimental.pallas import tpu as pltpu

F32 = jnp.float32
BF16 = jnp.bfloat16

D_MODEL = 2048
HEAD = 64
A_W = 1024
B_W = 1024
B_KVW = 256
B_GROUP = 4
LORA = 64
SHIFT_W = 3 * A_W + 2 * LORA
WINDOW = 128
SWA_Q_TILES_PER_KV_TILE = 4
ROPE_DIM = 16
ROPE_THETA = 500000.0
SGU_CHUNK = 128
SGU_GROUPS = 16
RMS_EPS = 1e-5
LN_EPS = 1e-5
LN_X_EPS = HEAD * 1e-5

LANES = 128
WKV_CHUNK = 64
WKV_GROUP = 4
WKV_STEP_CHUNKS = 2
VMEM_LIMIT = 56 * 1024 * 1024


def _cparams(n_axes):
    return pltpu.CompilerParams(
        dimension_semantics=("arbitrary",) * n_axes, vmem_limit_bytes=VMEM_LIMIT)


def _bdot(a, b):
    return jnp.dot(a.astype(BF16), b.astype(BF16), preferred_element_type=F32)


def _bdot_nt(a, b):
    return lax.dot_general(a.astype(BF16), b.astype(BF16), (((1,), (1,)), ((), ())),
                           preferred_element_type=F32)


def _bdot_tn(a, b):
    return lax.dot_general(a.astype(BF16), b.astype(BF16), (((0,), (0,)), ((), ())),
                           preferred_element_type=F32)


def _split_dot(x, ones_bf16, passes):
    acc = None
    rem = x
    for _ in range(passes):
        piece = rem.astype(BF16)
        term = jnp.dot(piece, ones_bf16, preferred_element_type=F32)
        acc = term if acc is None else acc + term
        rem = rem - piece.astype(F32)
    return acc


def _silu(x):
    return x * (1.0 / (1.0 + jnp.exp(-x)))


def _sigmoid(x):
    return 1.0 / (1.0 + jnp.exp(-x))


def _head_masks():
    lane = lax.broadcasted_iota(jnp.int32, (1, LANES), 1)
    return lane < HEAD, lane >= HEAD


def _norm_matmul_kernel(x_ref, g_ref, w_ref, *out_refs, seg_widths, col_chunk):
    x = x_ref[...]
    ms = jnp.mean(x * x, axis=-1, keepdims=True)
    h = ((x * lax.rsqrt(ms + RMS_EPS)) * g_ref[...]).astype(BF16)
    col = 0
    for o_ref, width in zip(out_refs, seg_widths):
        for c0 in range(0, width, col_chunk):
            cw = min(col_chunk, width - c0)
            o_ref[:, c0:c0 + cw] = jnp.dot(
                h, w_ref[:, col + c0:col + c0 + cw], preferred_element_type=F32)
        col += width


def _norm_matmul(x2d, g, w_bf16, seg_widths, tm):
    n, d = x2d.shape
    ncol = w_bf16.shape[1]
    assert sum(seg_widths) == ncol and n % tm == 0
    kern = functools.partial(_norm_matmul_kernel, seg_widths=tuple(seg_widths), col_chunk=512)
    return pl.pallas_call(
        kern,
        grid=(n // tm,),
        in_specs=[
            pl.BlockSpec((tm, d), lambda i: (i, 0)),
            pl.BlockSpec((1, d), lambda i: (0, 0)),
            pl.BlockSpec((d, ncol), lambda i: (0, 0), pipeline_mode=pl.Buffered(1)),
        ],
        out_specs=[pl.BlockSpec((tm, w), lambda i: (i, 0)) for w in seg_widths],
        out_shape=[jax.ShapeDtypeStruct((n, w), F32) for w in seg_widths],
        compiler_params=_cparams(1),
        name="norm_matmul",
    )(x2d, g.reshape(1, d), w_bf16)


def _prep_swa_kernel(*refs, has_vres):
    (x_ref, g_ref, w_ref, mu_ref, w0_ref, a0_ref, wa2_ref, kk_ref, ka_ref, tri_ref, ones_ref) = refs[:11]
    if has_vres:
        vf_ref, v0_ref, v1_ref, v2_ref = refs[11:15]
    (q_ref, kvc_ref, kvp_ref, cosc_ref, sinc_ref, cosp_ref, sinp_ref, bg_ref, sink_ref,
     r_out, k_out, v_out, na_out, nb_out, lw_out, cs_out, yb_out, carry_ref) = refs[-18:]

    @pl.when(pl.program_id(1) == 0)
    def _():
        carry_ref[...] = jnp.zeros_like(carry_ref)

    x = x_ref[0]
    tt = x.shape[0]

    consts = _swa_consts()
    first8 = consts[2]
    blk = WINDOW
    n_blk = tt // blk
    k_roped, v_rows = [], []
    for j in range(n_blk + 1):
        if j == 0:
            kv, cos_j, sin_j = kvp_ref[0], cosp_ref[0], sinp_ref[0]
        else:
            rows = slice((j - 1) * blk, j * blk)
            kv, cos_j, sin_j = kvc_ref[0, rows], cosc_ref[0, rows], sinc_ref[0, rows]
        k_roped.append([_rope(kv[:, _lane_tile(kt)], cos_j, sin_j, first8)
                        for kt in range(B_KVW // LANES)])
        v_rows.append([kv[:, B_KVW + kt * LANES:B_KVW + (kt + 1) * LANES]
                       for kt in range(B_KVW // LANES)])
    attn = []
    for j in range(n_blk):
        rows = slice(j * blk, (j + 1) * blk)
        k_win = [jnp.concatenate([k_roped[j][kt], k_roped[j + 1][kt]], axis=0)
                 for kt in range(B_KVW // LANES)]
        v_win = [jnp.concatenate([v_rows[j][kt], v_rows[j + 1][kt]], axis=0)
                 for kt in range(B_KVW // LANES)]
        attn.append(_swa_scores(lambda qt, rows=rows: q_ref[0, rows, _lane_tile(qt)], k_win, v_win,
                                cosc_ref[0, rows], sinc_ref[0, rows], consts))

    ms = jnp.mean(x * x, axis=-1, keepdims=True)
    h = ((x * lax.rsqrt(ms + RMS_EPS)) * g_ref[...]).astype(BF16)
    row = lax.broadcasted_iota(jnp.int32, (tt, 1), 0)

    def shifted(c0, width):
        z = jnp.dot(h, w_ref[:, c0:c0 + width], preferred_element_type=F32)
        prev = jnp.where(row == 0, carry_ref[0:1, c0:c0 + width], pltpu.roll(z, 1, axis=0))
        carry_ref[0:1, c0:c0 + width] = z[tt - 1:tt, :]
        return z + (prev - z) * mu_ref[:, c0:c0 + width]

    xwa = shifted(3 * A_W, 2 * LORA)
    lane = lax.broadcasted_iota(jnp.int32, (1, 2 * LORA), 1)
    lora_in = jnp.where(lane < LORA, jnp.tanh(xwa), xwa)
    lora = _bdot(lora_in, wa2_ref[...])
    wpre = w0_ref[...] + lora[:, 0:A_W]
    neg = -wpre
    softplus = jnp.maximum(neg, 0.0) + jnp.log(1.0 + jnp.exp(-jnp.abs(neg)))
    w_log = -softplus - 0.5
    lw = -jnp.exp(w_log)
    a = _sigmoid(a0_ref[...] + lora[:, A_W:2 * A_W])
    lw_out[0] = lw
    r_out[0] = shifted(0, A_W)
    hi = lw.astype(BF16)
    lo = (lw - hi.astype(F32)).astype(BF16)
    tri = tri_ref[...]
    cs_out[0] = (jnp.dot(tri, hi, preferred_element_type=F32)
                 + jnp.dot(tri, lo, preferred_element_type=F32))
    k = shifted(A_W, A_W)
    k_out[0] = k * (1.0 + (a - 1.0) * ka_ref[...])
    kk = k * kk_ref[...]
    ones_bd = ones_ref[...]
    gw = ones_bd.shape[0]
    for p in range(A_W // gw):
        sl = slice(p * gw, (p + 1) * gw)
        kkp = kk[:, sl]
        ss = _bdot(kkp * kkp, ones_bd)
        kkn = kkp * lax.rsqrt(ss + 1e-12)
        na_out[0, :, sl] = -kkn
        nb_out[0, :, sl] = kkn * a[:, sl]
    v = shifted(2 * A_W, A_W)
    if has_vres:
        vf = vf_ref[0]
        gate = _sigmoid(v0_ref[...] + _bdot(_bdot(v, v1_ref[...]), v2_ref[...]))
        v = v + (vf - v) * gate
    v_out[0] = v

    for j in range(n_blk):
        rows = slice(j * blk, (j + 1) * blk)
        has_prev = (pl.program_id(1) > 0) if j == 0 else True

        def store_tile(qt, val, rows=rows):
            yb_out[0, rows, _lane_tile(qt)] = val

        _swa_outputs(attn[j][0], attn[j][1], has_prev, sink_ref,
                     lambda qt, rows=rows: bg_ref[0, rows, _lane_tile(qt)], store_tile, consts)


def _prep_swa(x3d, g, w_a, mu, w0, w2, a0, a2, k_k, k_a, vres, q, kv, cos_t, sin_t, b_gate, sinks, tt):
    b, t, d = x3d.shape
    has_vres = vres is not None
    wa2 = jnp.zeros((2 * LORA, 2 * A_W), F32)
    wa2 = wa2.at[:LORA, :A_W].set(w2).at[LORA:, A_W:].set(a2).astype(BF16)
    ti = jnp.arange(tt)
    tri = ((ti[:, None] >= ti[None, :]) &
           (ti[:, None] // WKV_CHUNK == ti[None, :] // WKV_CHUNK)).astype(BF16)
    gw = WKV_GROUP * HEAD
    li = jnp.arange(gw)
    ones_bd = (li[:, None] // HEAD == li[None, :] // HEAD).astype(BF16)
    row = lambda p: p.reshape(1, -1)
    full = lambda shape: pl.BlockSpec(shape, lambda bi, i: (0,) * len(shape))
    tile = lambda w: pl.BlockSpec((1, tt, w), lambda bi, i: (bi, i, 0))
    in_specs = [
        tile(d), full((1, d)),
        pl.BlockSpec((d, SHIFT_W), lambda bi, i: (0, 0), pipeline_mode=pl.Buffered(1)),
        full((1, SHIFT_W)), full((1, A_W)), full((1, A_W)), full((2 * LORA, 2 * A_W)),
        full((1, A_W)), full((1, A_W)), full((tt, tt)), full((gw, gw)),
    ]
    args = [x3d, row(g), w_a, row(mu), row(w0), row(a0), wa2, row(k_k), row(k_a), tri, ones_bd]
    if has_vres:
        v_first, v0, v1, v2 = vres
        in_specs += [tile(A_W), full((1, A_W)), full(v1.shape), full(v2.shape)]
        args += [v_first, row(v0), v1.astype(BF16), v2.astype(BF16)]
    blk = WINDOW
    prev_blk = lambda w: pl.BlockSpec(
        (1, blk, w), lambda bi, i: (bi, jnp.maximum(i * (tt // blk) - 1, 0), 0))
    in_specs += [tile(B_W), tile(2 * B_KVW), prev_blk(2 * B_KVW), tile(LANES), tile(LANES),
                 prev_blk(LANES), prev_blk(LANES), tile(B_W), full((1, B_W))]
    args += [q, kv, kv, cos_t, sin_t, cos_t, sin_t, b_gate,
             jnp.repeat(sinks.astype(F32), HEAD).reshape(1, B_W)]
    return pl.pallas_call(
        functools.partial(_prep_swa_kernel, has_vres=has_vres),
        grid=(b, t // tt),
        in_specs=in_specs,
        out_specs=[tile(A_W)] * 7 + [tile(B_W)],
        out_shape=[jax.ShapeDtypeStruct((b, t, A_W), F32)] * 7 + [jax.ShapeDtypeStruct((b, t, B_W), F32)],
        scratch_shapes=[pltpu.VMEM((8, SHIFT_W), F32)],
        compiler_params=_cparams(2),
        name="prep_swa",
    )(*args)


def _wkv_kernel(r_ref, k_ref, v_ref, na_ref, nb_ref, lw_ref, cs_ref, g_ref,
                lnw_ref, lnb_ref, rk_ref, ones_ref, o_ref, s_ref):
    c = WKV_CHUNK

    @pl.when(pl.program_id(0) == 0)
    def _():
        s_ref[...] = jnp.zeros_like(s_ref)

    gw = WKV_GROUP * HEAD
    lane = lax.broadcasted_iota(jnp.int32, (1, gw), 1)
    head_masks = [(lane >= h * HEAD) & (lane < (h + 1) * HEAD) for h in range(WKV_GROUP)]
    ti = lax.broadcasted_iota(jnp.int32, (c, gw), 0)
    tj = lax.broadcasted_iota(jnp.int32, (c, gw), 1) & (c - 1)
    strict = ti > tj
    incl = ti >= tj
    eye_t = jnp.where(ti == tj, 1.0, 0.0)
    li = lax.broadcasted_iota(jnp.int32, (gw, gw), 0)
    lj = lax.broadcasted_iota(jnp.int32, (gw, gw), 1)
    bdmask = (li // HEAD) == (lj // HEAD)
    ones_bd = ones_ref[...]

    def bd(x):
        xb = x.astype(BF16)
        zero = jnp.zeros_like(xb)
        return jnp.concatenate([jnp.where(m, xb, zero) for m in head_masks], axis=0)

    n_ck = r_ref.shape[1] // c
    n_state = s_ref.shape[0]
    chains = [(bi, slice(ck * c, (ck + 1) * c), slice(p * gw, (p + 1) * gw))
              for ck in range(n_ck) for bi in range(r_ref.shape[0]) for p in range(A_W // gw)]
    pairs = range(len(chains))
    rd = lambda ref, p: ref[chains[p]]
    rt, at_abs, bh, kh, wc, em, lhs0, rhs0 = [], [], [], [], [], [], [], []
    for p in pairs:
        cs = rd(cs_ref, p)
        mid = cs[c // 2 - 1:c // 2, :]
        last = cs[c - 1:c, :]
        e_neg = jnp.exp(mid - cs)
        rt_p = rd(r_ref, p) * jnp.exp(cs - mid)
        at_p = rd(na_ref, p) * jnp.exp(cs - rd(lw_ref, p) - mid)
        kt_p = rd(k_ref, p) * e_neg
        bt_p = rd(nb_ref, p) * e_neg
        em_p = jnp.exp(mid)
        e_end = jnp.exp(last - mid)
        rt.append(rt_p)
        at_abs.append(at_p * em_p)
        em.append(em_p)
        wc.append(jnp.exp(last))
        bh.append(bt_p * e_end)
        kh.append(kt_p * e_end)
        lhs0.append(jnp.concatenate([at_p, rt_p], axis=0))
        rhs0.append(jnp.concatenate([bd(bt_p), bd(kt_p)], axis=0))
    sc = [_bdot_nt(lhs0[p], rhs0[p]) for p in pairs]
    a_ab = [jnp.where(strict, sc[p][:c, :gw], 0.0) for p in pairs]
    a_rb = [jnp.where(incl, sc[p][c:, :gw], 0.0) for p in pairs]
    akrk = [jnp.concatenate([jnp.where(strict, sc[p][:c, gw:], 0.0),
                             jnp.where(incl, sc[p][c:, gw:], 0.0)], axis=0) for p in pairs]
    xv = [_bdot(akrk[p], bd(rd(v_ref, p))) for p in pairs]
    pinv = [eye_t + a_ab[p] for p in pairs]
    apow = [_bdot(a_ab[p], bd(a_ab[p])) for p in pairs]
    for _ in range(4):
        both = [_bdot(jnp.concatenate([apow[p], pinv[p]], axis=0), bd(apow[p])) for p in pairs]
        pinv = [pinv[p] + both[p][c:] for p in pairs]
        apow = [both[p][:c] for p in pairs]
    pinv = [pinv[p] + _bdot(pinv[p], bd(apow[p])) for p in pairs]
    ta = [_bdot(pinv[p], jnp.concatenate([bd(xv[p][:c]), bd(at_abs[p])], axis=1)) for p in pairs]
    uv = [ta[p][:, :gw] for p in pairs]
    ap = [ta[p][:, gw:] for p in pairs]
    rb = [_bdot(a_rb[p], jnp.concatenate([bd(ap[p]), bd(uv[p])], axis=1)) for p in pairs]
    rpap = [jnp.concatenate([rt[p] * em[p] + rb[p][:, :gw], ap[p]], axis=0) for p in pairs]
    yv = [rb[p][:, gw:] + xv[p][c:] for p in pairs]
    bk = [jnp.concatenate([bh[p], kh[p]], axis=0) for p in pairs]
    state = [s_ref[si] for si in range(n_state)]
    y = []
    for ck in range(n_ck):
        ps = [ck * n_state + si for si in range(n_state)]
        ru = [_bdot_nt(rpap[p], state[si]) for si, p in enumerate(ps)]
        y += [ru[si][:c] + yv[p] for si, p in enumerate(ps)]
        grams = [_bdot_tn(jnp.concatenate([ru[si][c:] + uv[p], rd(v_ref, p)], axis=0), bk[p])
                 for si, p in enumerate(ps)]
        state = [state[si] * wc[p] + jnp.where(bdmask, grams[si], 0.0) for si, p in enumerate(ps)]
    for si in range(n_state):
        s_ref[si] = state[si]
    stat_rows = []
    for p in pairs:
        stat_rows += [y[p], rd(r_ref, p) * rd(k_ref, p) * rk_ref[:, chains[p][2]]]
    stats_all = _bdot(jnp.concatenate(stat_rows, axis=0), ones_bd)
    stats = [stats_all[2 * c * p:2 * c * (p + 1)] for p in pairs]
    d = [y[p] - stats[p][:c] * (1.0 / HEAD) for p in pairs]
    var_all = _bdot(jnp.concatenate([d[p] * d[p] for p in pairs], axis=0), ones_bd) * (1.0 / HEAD)
    var = [var_all[c * p:c * (p + 1)] for p in pairs]
    for p in pairs:
        sl = chains[p][2]
        yn = d[p] * lax.rsqrt(var[p] + LN_X_EPS) * lnw_ref[:, sl] + lnb_ref[:, sl]
        out = yn + stats[p][c:] * rd(v_ref, p)
        o_ref[chains[p]] = out * _silu(rd(g_ref, p))


def _wkv(r, k, v, na, nb, lw, cs, gate, ln_w, ln_b, r_k):
    b, t, _ = r.shape
    c = WKV_CHUNK
    gw = WKV_GROUP * HEAD
    li = jnp.arange(gw)
    ones_bd = (li[:, None] // HEAD == li[None, :] // HEAD).astype(BF16)
    rows = WKV_STEP_CHUNKS * c
    assert t % rows == 0
    tile = pl.BlockSpec((b, rows, A_W), lambda i: (0, i, 0))
    vec = pl.BlockSpec((1, A_W), lambda i: (0, 0))
    return pl.pallas_call(
        _wkv_kernel,
        grid=(t // rows,),
        in_specs=[tile] * 8 + [vec] * 3 + [pl.BlockSpec((gw, gw), lambda i: (0, 0))],
        out_specs=tile,
        out_shape=jax.ShapeDtypeStruct((b, t, A_W), F32),
        scratch_shapes=[pltpu.VMEM((b * (A_W // gw), gw, gw), F32)],
        compiler_params=_cparams(1),
        name="wkv",
    )(r, k, v, na, nb, lw, cs, gate, ln_w.reshape(1, A_W), ln_b.reshape(1, A_W),
      r_k.reshape(1, A_W), ones_bd)


def _rope_angles_kernel(pos_ref, freq_ref, cos_ref, sin_ref, nsin_ref):
    ang = pos_ref[...].astype(F32) * freq_ref[...]
    sin = jnp.sin(ang)
    cos_ref[...] = jnp.cos(ang)
    sin_ref[...] = sin
    nsin_ref[...] = -sin


def _rope_tables(positions):
    b, t = positions.shape
    half = ROPE_DIM // 2
    n_rows = b * t * half // LANES
    inv_freq = jnp.power(jnp.float32(ROPE_THETA), -jnp.arange(half, dtype=F32) / half)
    pos_rep = jnp.repeat(positions.reshape(-1), half).reshape(n_rows, LANES)
    freq = jnp.tile(inv_freq, LANES // half).reshape(1, LANES)
    whole = pl.BlockSpec((n_rows, LANES), lambda i: (0, 0))
    cos_c, sin_c, nsin_c = pl.pallas_call(
        _rope_angles_kernel,
        grid=(1,),
        in_specs=[whole, pl.BlockSpec((1, LANES), lambda i: (0, 0))],
        out_specs=[whole] * 3,
        out_shape=[jax.ShapeDtypeStruct((n_rows, LANES), F32)] * 3,
        compiler_params=_cparams(1),
        name="rope_angles",
    )(pos_rep, freq)
    cos8, sin8, nsin8 = (z.reshape(b, t, half) for z in (cos_c, sin_c, nsin_c))
    rest = HEAD - ROPE_DIM
    cos_head = jnp.concatenate([cos8, cos8, jnp.ones((b, t, rest), F32)], axis=-1)
    sin_head = jnp.concatenate([nsin8, sin8, jnp.zeros((b, t, rest), F32)], axis=-1)
    n_heads = LANES // HEAD
    return jnp.tile(cos_head, (1, 1, n_heads)), jnp.tile(sin_head, (1, 1, n_heads))


def _rope(x, cos_t, sin_t, first8):
    fwd = pltpu.roll(x, LANES - ROPE_DIM // 2, axis=1)
    bwd = pltpu.roll(x, ROPE_DIM // 2, axis=1)
    return x * cos_t + jnp.where(first8, fwd, bwd) * sin_t


def _lane_tile(j):
    return slice(j * LANES, (j + 1) * LANES)


def _swa_consts():
    blk = WINDOW
    m0, m1 = _head_masks()
    lane = lax.broadcasted_iota(jnp.int32, (1, LANES), 1)
    first8 = (lane & (HEAD - 1)) < ROPE_DIM // 2
    ki = lax.broadcasted_iota(jnp.int32, (2 * blk, blk), 0)
    qi = lax.broadcasted_iota(jnp.int32, (2 * blk, blk), 1) + blk
    band = (ki <= qi) & (qi - ki < WINDOW)
    in_cur = ki >= blk
    return m0, m1, first8, band, in_cur


def _swa_scores(q_tile_at, k_win, v_win, cos_c, sin_c, consts):
    m0, m1, first8, _, _ = consts
    scale = 1.0 / math.sqrt(HEAD)
    scores_t, values_t = [], []
    for kt in range(B_KVW // LANES):
        values_t.append(v_win[kt].T)
        rows = []
        for j in range(SWA_Q_TILES_PER_KV_TILE):
            qp = _rope(q_tile_at(kt * SWA_Q_TILES_PER_KV_TILE + j), cos_c, sin_c, first8) * scale
            rows += [jnp.where(m0, qp, 0.0), jnp.where(m1, qp, 0.0)]
        scores_t.append(_bdot_nt(k_win[kt], jnp.concatenate(rows, axis=0)))
    return scores_t, values_t


def _swa_outputs(scores_t, values_t, has_prev, sink_ref, gate_tile_at, store_tile, consts):
    _, _, _, band, in_cur = consts
    blk = WINDOW
    valid = band & (has_prev | in_cur)
    ones_rows = jnp.ones((16, 2 * blk), BF16)
    for kt in range(B_KVW // LANES):
        for j in range(SWA_Q_TILES_PER_KV_TILE):
            qt = kt * SWA_Q_TILES_PER_KV_TILE + j
            halves = []
            for half in range(2):
                c0 = (2 * j + half) * blk
                s = jnp.where(valid, scores_t[kt][:, c0:c0 + blk], -jnp.inf)
                sink = sink_ref[:, qt * LANES + half * HEAD:qt * LANES + half * HEAD + 1]
                m = jnp.maximum(jnp.max(s, axis=0, keepdims=True), sink)
                p = jnp.exp(s - m).astype(BF16)
                lhs = jnp.concatenate(
                    [values_t[kt][half * HEAD:(half + 1) * HEAD].astype(BF16), ones_rows], axis=0)
                od = jnp.dot(lhs, p, preferred_element_type=F32)
                den = od[HEAD:HEAD + 1] + jnp.exp(sink - m)
                halves.append(od[:HEAD] * (1.0 / den))
            o_tile = jnp.concatenate(halves, axis=0).T
            store_tile(qt, o_tile * _silu(gate_tile_at(qt)))


def _swa_head_perm(z, axis):
    per = z.shape[axis] // (B_W // HEAD)
    shape = z.shape[:axis] + (2, 2, B_GROUP, per) + z.shape[axis + 1:]
    order = list(range(len(shape)))
    order[axis + 1], order[axis + 2] = axis + 2, axis + 1
    return jnp.transpose(z.reshape(shape), order).reshape(z.shape)


def _out_proj_kernel(x_ref, ya_ref, yb_ref, w_ref, o_ref):
    y = jnp.concatenate([ya_ref[...].astype(BF16), yb_ref[...].astype(BF16)], axis=1)
    o_ref[...] = x_ref[...] + jnp.dot(y, w_ref[...], preferred_element_type=F32)


def _out_proj(x2d, ya, yb, w_bf16, tm):
    n, d = x2d.shape
    return pl.pallas_call(
        _out_proj_kernel,
        grid=(n // tm,),
        in_specs=[
            pl.BlockSpec((tm, d), lambda i: (i, 0)),
            pl.BlockSpec((tm, A_W), lambda i: (i, 0)),
            pl.BlockSpec((tm, B_W), lambda i: (i, 0)),
            pl.BlockSpec((A_W + B_W, d), lambda i: (0, 0), pipeline_mode=pl.Buffered(1)),
        ],
        out_specs=pl.BlockSpec((tm, d), lambda i: (i, 0)),
        out_shape=jax.ShapeDtypeStruct((n, d), F32),
        compiler_params=_cparams(1),
        name="out_proj",
    )(x2d, ya, yb, w_bf16)


def _odd_layer_kernel(x_ref, g_ref, win_ref, lnw_ref, lnb_ref, ws_ref, bst_ref, wout_ref,
                      fin_ref, o_ref, y_ref, *, final_norm):
    tm, d = x_ref.shape
    ch = SGU_CHUNK
    x = x_ref[...]
    ms = jnp.mean(x * x, axis=-1, keepdims=True)
    h = ((x * lax.rsqrt(ms + RMS_EPS)) * g_ref[...]).astype(BF16)
    v = jnp.dot(h, win_ref[:, d:2 * d], preferred_element_type=F32)
    mean = jnp.mean(v, axis=-1, keepdims=True)
    dv = v - mean
    var = jnp.mean(dv * dv, axis=-1, keepdims=True)
    vn = ((dv * lax.rsqrt(var + LN_EPS)) * lnw_ref[...] + lnb_ref[...]).astype(BF16)
    ti = lax.broadcasted_iota(jnp.int32, (ch, ch), 0)
    si = lax.broadcasted_iota(jnp.int32, (ch, ch), 1)
    causal = ti >= si
    cw = 2 * LANES
    zero_tile = jnp.zeros((ch, LANES), BF16)
    for j in range(d // cw):
        c0 = j * cw
        u = jnp.dot(h, win_ref[:, c0:c0 + cw], preferred_element_type=F32)
        gate = jnp.dot(h, win_ref[:, 2 * d + c0:2 * d + c0 + cw], preferred_element_type=F32)
        wm = jnp.concatenate(
            [jnp.where(causal, ws_ref[2 * j + gi], 0.0).astype(BF16) for gi in range(2)], axis=1)
        bias = jnp.concatenate(
            [jnp.broadcast_to(bst_ref[:, 2 * j + gi:2 * j + gi + 1], (ch, LANES)) for gi in range(2)],
            axis=1)
        mixed = []
        for ci in range(tm // ch):
            vc = vn[ci * ch:(ci + 1) * ch, c0:c0 + cw]
            v_bd = jnp.concatenate(
                [jnp.concatenate([vc[:, :LANES], zero_tile], axis=1),
                 jnp.concatenate([zero_tile, vc[:, LANES:]], axis=1)], axis=0)
            mixed.append(jnp.dot(wm, v_bd, preferred_element_type=F32) + bias)
        y_ref[:, c0:c0 + cw] = (u * jnp.concatenate(mixed, axis=0) * _silu(gate)).astype(BF16)
    out = x + jnp.dot(y_ref[...], wout_ref[...], preferred_element_type=F32)
    if final_norm:
        ms = jnp.mean(out * out, axis=-1, keepdims=True)
        out = (out * lax.rsqrt(ms + RMS_EPS)) * fin_ref[...]
    o_ref[...] = out


def _odd_layer(x2d, g, w_in_bf16, ln_w, ln_b, ws, bs, w_out_bf16, final_g, tm):
    n, d = x2d.shape
    final_norm = final_g is not None
    fin = final_g if final_norm else jnp.ones((d,), F32)
    row_tile = pl.BlockSpec((tm, d), lambda i: (i, 0))
    vec = pl.BlockSpec((1, d), lambda i: (0, 0))
    return pl.pallas_call(
        functools.partial(_odd_layer_kernel, final_norm=final_norm),
        grid=(n // tm,),
        in_specs=[
            row_tile, vec,
            pl.BlockSpec((d, 3 * d), lambda i: (0, 0), pipeline_mode=pl.Buffered(1)),
            vec, vec,
            pl.BlockSpec((SGU_GROUPS, SGU_CHUNK, SGU_CHUNK), lambda i: (0, 0, 0)),
            pl.BlockSpec((SGU_CHUNK, SGU_GROUPS), lambda i: (0, 0)),
            pl.BlockSpec((d, d), lambda i: (0, 0), pipeline_mode=pl.Buffered(1)),
            vec,
        ],
        out_specs=row_tile,
        out_shape=jax.ShapeDtypeStruct((n, d), F32),
        scratch_shapes=[pltpu.VMEM((tm, d), BF16)],
        compiler_params=_cparams(1),
        name="odd_layer",
    )(x2d, g.reshape(1, d), w_in_bf16, ln_w.reshape(1, d), ln_b.reshape(1, d), ws, bs.T,
      w_out_bf16, fin.reshape(1, d))


def kernel(x, positions, e_norm, e_w_in, e_mu, rwkv_w0, rwkv_w2, rwkv_a0, rwkv_a2, rwkv_k_k, rwkv_k_a, rwkv_r_k, rwkv_ln_w, rwkv_ln_b, rwkv_v0, rwkv_v1, rwkv_v2, attn_sinks, e_w_out, o_norm, o_w_in, sgu_ln_w, sgu_ln_b, sgu_ws, sgu_bs, o_w_out, final_norm):
    b, t, d = x.shape
    n = b * t
    depth = e_norm.shape[0] + o_norm.shape[0]
    assert t % 256 == 0 and d == D_MODEL and depth % 2 == 0
    x2d = x.reshape(n, d)
    cos_t, sin_t = _rope_tables(positions)
    v_first = None
    for layer in range(depth):
        if layer % 2 == 0:
            e = layer // 2
            w_a = e_w_in[e, :, :SHIFT_W].astype(BF16)
            w_b = e_w_in[e, :, SHIFT_W:]
            w_rest = jnp.concatenate(
                [w_b[:, :A_W], _swa_head_perm(w_b[:, A_W:A_W + B_W], 1),
                 w_b[:, A_W + B_W:A_W + B_W + 2 * B_KVW], _swa_head_perm(w_b[:, -B_W:], 1)],
                axis=1).astype(BF16)
            sinks = _swa_head_perm(attn_sinks[e], 0)
            w_out = jnp.concatenate(
                [e_w_out[e, :A_W], _swa_head_perm(e_w_out[e, A_W:], 0)], axis=0).astype(BF16)
            r3 = lambda z: z.reshape(b, t, z.shape[-1])
            vres = None if e == 0 else (v_first, rwkv_v0[e - 1], rwkv_v1[e - 1], rwkv_v2[e - 1])
            a_gate, q, kv, b_gate = _norm_matmul(
                x2d, e_norm[e], w_rest, (A_W, B_W, 2 * B_KVW, B_W), tm=512)
            r, k2, v, na, nb, lw, cs, yb = _prep_swa(
                r3(x2d), e_norm[e], w_a, e_mu[e], rwkv_w0[e], rwkv_w2[e], rwkv_a0[e],
                rwkv_a2[e], rwkv_k_k[e], rwkv_k_a[e], vres, r3(q), r3(kv), cos_t, sin_t, r3(b_gate),
                sinks, tt=256)
            if e == 0:
                v_first = v
            ya = _wkv(r, k2, v, na, nb, lw, cs, r3(a_gate), rwkv_ln_w[e], rwkv_ln_b[e], rwkv_r_k[e])
            x2d = _out_proj(x2d, ya.reshape(n, A_W), yb.reshape(n, B_W), w_out, tm=512)
        else:
            o = layer // 2
            fin = final_norm if layer == depth - 1 else None
            x2d = _odd_layer(x2d, o_norm[o], o_w_in[o].astype(BF16), sgu_ln_w[o], sgu_ln_b[o],
                             sgu_ws[o], sgu_bs[o], o_w_out[o].astype(BF16), fin, tm=256)
    return x2d.reshape(b, t, d)
```

```python
import functools
import math

import jax
import jax.numpy as jnp
from jax import lax
from jax.experimental import pallas as pl
from jax.experimental.pallas import tpu as pltpu

F32 = jnp.float32
BF16 = jnp.bfloat16

D_MODEL = 2048
HEAD = 64
A_W = 1024
B_W = 1024
B_KVW = 256
B_GROUP = 4
LORA = 64
SHIFT_W = 3 * A_W + 2 * LORA
WINDOW = 128
ROPE_DIM = 16
ROPE_THETA = 500000.0
SGU_CHUNK = 128
SGU_GROUPS = 16
RMS_EPS = 1e-5
LN_EPS = 1e-5
LN_X_EPS = HEAD * 1e-5

LANES = 128
WKV_CHUNK = 64
WKV_GROUP = 4
WKV_STEP_CHUNKS = 2
VMEM_LIMIT = 56 * 1024 * 1024


def _cparams(n_axes):
    return pltpu.CompilerParams(
        dimension_semantics=("arbitrary",) * n_axes, vmem_limit_bytes=VMEM_LIMIT)


def _bdot(a, b):
    return jnp.dot(a.astype(BF16), b.astype(BF16), preferred_element_type=F32)


def _bdot_nt(a, b):
    return lax.dot_general(a.astype(BF16), b.astype(BF16), (((1,), (1,)), ((), ())),
                           preferred_element_type=F32)


def _bdot_tn(a, b):
    return lax.dot_general(a.astype(BF16), b.astype(BF16), (((0,), (0,)), ((), ())),
                           preferred_element_type=F32)


def _split_dot(x, ones_bf16, passes):
    acc = None
    rem = x
    for _ in range(passes):
        piece = rem.astype(BF16)
        term = jnp.dot(piece, ones_bf16, preferred_element_type=F32)
        acc = term if acc is None else acc + term
        rem = rem - piece.astype(F32)
    return acc


def _silu(x):
    return x * (1.0 / (1.0 + jnp.exp(-x)))


def _sigmoid(x):
    return 1.0 / (1.0 + jnp.exp(-x))


def _head_masks():
    lane = lax.broadcasted_iota(jnp.int32, (1, LANES), 1)
    return lane < HEAD, lane >= HEAD


def _norm_matmul_kernel(x_ref, g_ref, w_ref, *out_refs, seg_widths, col_chunk):
    x = x_ref[...]
    ms = jnp.mean(x * x, axis=-1, keepdims=True)
    h = ((x * lax.rsqrt(ms + RMS_EPS)) * g_ref[...]).astype(BF16)
    col = 0
    for o_ref, width in zip(out_refs, seg_widths):
        for c0 in range(0, width, col_chunk):
            cw = min(col_chunk, width - c0)
            o_ref[:, c0:c0 + cw] = jnp.dot(
                h, w_ref[:, col + c0:col + c0 + cw], preferred_element_type=F32)
        col += width


def _norm_matmul(x2d, g, w_bf16, seg_widths, tm):
    n, d = x2d.shape
    ncol = w_bf16.shape[1]
    assert sum(seg_widths) == ncol and n % tm == 0
    kern = functools.partial(_norm_matmul_kernel, seg_widths=tuple(seg_widths), col_chunk=512)
    return pl.pallas_call(
        kern,
        grid=(n // tm,),
        in_specs=[
            pl.BlockSpec((tm, d), lambda i: (i, 0)),
            pl.BlockSpec((1, d), lambda i: (0, 0)),
            pl.BlockSpec((d, ncol), lambda i: (0, 0), pipeline_mode=pl.Buffered(1)),
        ],
        out_specs=[pl.BlockSpec((tm, w), lambda i: (i, 0)) for w in seg_widths],
        out_shape=[jax.ShapeDtypeStruct((n, w), F32) for w in seg_widths],
        compiler_params=_cparams(1),
        name="norm_matmul",
    )(x2d, g.reshape(1, d), w_bf16)


def _prep_swa_kernel(*refs, has_vres):
    (x_ref, g_ref, w_ref, mu_ref, w0_ref, a0_ref, wa2_ref, kk_ref, ka_ref, tri_ref, ones_ref) = refs[:11]
    if has_vres:
        vf_ref, v0_ref, v1_ref, v2_ref = refs[11:15]
    (q_ref, kvc_ref, kvp_ref, cosc_ref, sinc_ref, cosp_ref, sinp_ref, bg_ref, sink_ref,
     r_out, k_out, v_out, na_out, nb_out, lw_out, cs_out, yb_out, carry_ref) = refs[-18:]

    @pl.when(pl.program_id(1) == 0)
    def _():
        carry_ref[...] = jnp.zeros_like(carry_ref)

    x = x_ref[0]
    tt = x.shape[0]

    consts = _swa_consts()
    first8 = consts[2]
    blk = WINDOW
    n_blk = tt // blk
    k_roped, v_rows = [], []
    for j in range(n_blk + 1):
        if j == 0:
            kv, cos_j, sin_j = kvp_ref[0], cosp_ref[0], sinp_ref[0]
        else:
            rows = slice((j - 1) * blk, j * blk)
            kv, cos_j, sin_j = kvc_ref[0, rows], cosc_ref[0, rows], sinc_ref[0, rows]
        k_roped.append([_rope(kv[:, _lane_tile(kt)], cos_j, sin_j, first8)
                        for kt in range(B_KVW // LANES)])
        v_rows.append([kv[:, B_KVW + kt * LANES:B_KVW + (kt + 1) * LANES]
                       for kt in range(B_KVW // LANES)])
    attn = []
    for j in range(n_blk):
        rows = slice(j * blk, (j + 1) * blk)
        k_win = [jnp.concatenate([k_roped[j][kt], k_roped[j + 1][kt]], axis=0)
                 for kt in range(B_KVW // LANES)]
        v_win = [jnp.concatenate([v_rows[j][kt], v_rows[j + 1][kt]], axis=0)
                 for kt in range(B_KVW // LANES)]
        attn.append(_swa_scores(lambda qt, rows=rows: q_ref[0, rows, _lane_tile(qt)], k_win, v_win,
                                cosc_ref[0, rows], sinc_ref[0, rows], consts))

    ms = jnp.mean(x * x, axis=-1, keepdims=True)
    h = ((x * lax.rsqrt(ms + RMS_EPS)) * g_ref[...]).astype(BF16)
    row = lax.broadcasted_iota(jnp.int32, (tt, 1), 0)

    def shifted(c0, width):
        z = jnp.dot(h, w_ref[:, c0:c0 + width], preferred_element_type=F32)
        prev = jnp.where(row == 0, carry_ref[0:1, c0:c0 + width], pltpu.roll(z, 1, axis=0))
        carry_ref[0:1, c0:c0 + width] = z[tt - 1:tt, :]
        return z + (prev - z) * mu_ref[:, c0:c0 + width]

    xwa = shifted(3 * A_W, 2 * LORA)
    lane = lax.broadcasted_iota(jnp.int32, (1, 2 * LORA), 1)
    lora_in = jnp.where(lane < LORA, jnp.tanh(xwa), xwa)
    lora = _bdot(lora_in, wa2_ref[...])
    wpre = w0_ref[...] + lora[:, 0:A_W]
    neg = -wpre
    softplus = jnp.maximum(neg, 0.0) + jnp.log(1.0 + jnp.exp(-jnp.abs(neg)))
    w_log = -softplus - 0.5
    lw = -jnp.exp(w_log)
    a = _sigmoid(a0_ref[...] + lora[:, A_W:2 * A_W])
    lw_out[0] = lw
    r_out[0] = shifted(0, A_W)
    hi = lw.astype(BF16)
    lo = (lw - hi.astype(F32)).astype(BF16)
    tri = tri_ref[...]
    cs_out[0] = (jnp.dot(tri, hi, preferred_element_type=F32)
                 + jnp.dot(tri, lo, preferred_element_type=F32))
    k = shifted(A_W, A_W)
    k_out[0] = k * (1.0 + (a - 1.0) * ka_ref[...])
    kk = k * kk_ref[...]
    ones_bd = ones_ref[...]
    gw = ones_bd.shape[0]
    for p in range(A_W // gw):
        sl = slice(p * gw, (p + 1) * gw)
        kkp = kk[:, sl]
        ss = _bdot(kkp * kkp, ones_bd)
        kkn = kkp * lax.rsqrt(ss + 1e-12)
        na_out[0, :, sl] = -kkn
        nb_out[0, :, sl] = kkn * a[:, sl]
    v = shifted(2 * A_W, A_W)
    if has_vres:
        vf = vf_ref[0]
        gate = _sigmoid(v0_ref[...] + _bdot(_bdot(v, v1_ref[...]), v2_ref[...]))
        v = v + (vf - v) * gate
    v_out[0] = v

    for j in range(n_blk):
        rows = slice(j * blk, (j + 1) * blk)
        has_prev = (pl.program_id(1) > 0) if j == 0 else True

        def store_tile(qt, val, rows=rows):
            yb_out[0, rows, _lane_tile(qt)] = val

        _swa_outputs(attn[j][0], attn[j][1], has_prev, sink_ref,
                     lambda qt, rows=rows: bg_ref[0, rows, _lane_tile(qt)], store_tile, consts)


def _prep_swa(x3d, g, w_a, mu, w0, w2, a0, a2, k_k, k_a, vres, q, kv, cos_t, sin_t, b_gate, sinks, tt):
    b, t, d = x3d.shape
    has_vres = vres is not None
    wa2 = jnp.zeros((2 * LORA, 2 * A_W), F32)
    wa2 = wa2.at[:LORA, :A_W].set(w2).at[LORA:, A_W:].set(a2).astype(BF16)
    ti = jnp.arange(tt)
    tri = ((ti[:, None] >= ti[None, :]) &
           (ti[:, None] // WKV_CHUNK == ti[None, :] // WKV_CHUNK)).astype(BF16)
    gw = WKV_GROUP * HEAD
    li = jnp.arange(gw)
    ones_bd = (li[:, None] // HEAD == li[None, :] // HEAD).astype(BF16)
    row = lambda p: p.reshape(1, -1)
    full = lambda shape: pl.BlockSpec(shape, lambda bi, i: (0,) * len(shape))
    tile = lambda w: pl.BlockSpec((1, tt, w), lambda bi, i: (bi, i, 0))
    in_specs = [
        tile(d), full((1, d)),
        pl.BlockSpec((d, SHIFT_W), lambda bi, i: (0, 0), pipeline_mode=pl.Buffered(1)),
        full((1, SHIFT_W)), full((1, A_W)), full((1, A_W)), full((2 * LORA, 2 * A_W)),
        full((1, A_W)), full((1, A_W)), full((tt, tt)), full((gw, gw)),
    ]
    args = [x3d, row(g), w_a, row(mu), row(w0), row(a0), wa2, row(k_k), row(k_a), tri, ones_bd]
    if has_vres:
        v_first, v0, v1, v2 = vres
        in_specs += [tile(A_W), full((1, A_W)), full(v1.shape), full(v2.shape)]
        args += [v_first, row(v0), v1.astype(BF16), v2.astype(BF16)]
    blk = WINDOW
    prev_blk = lambda w: pl.BlockSpec(
        (1, blk, w), lambda bi, i: (bi, jnp.maximum(i * (tt // blk) - 1, 0), 0))
    in_specs += [tile(B_W), tile(2 * B_KVW), prev_blk(2 * B_KVW), tile(LANES), tile(LANES),
                 prev_blk(LANES), prev_blk(LANES), tile(B_W), full((1, B_W))]
    args += [q, kv, kv, cos_t, sin_t, cos_t, sin_t, b_gate,
             jnp.repeat(sinks.astype(F32), HEAD).reshape(1, B_W)]
    return pl.pallas_call(
        functools.partial(_prep_swa_kernel, has_vres=has_vres),
        grid=(b, t // tt),
        in_specs=in_specs,
        out_specs=[tile(A_W)] * 7 + [tile(B_W)],
        out_shape=[jax.ShapeDtypeStruct((b, t, A_W), F32)] * 7 + [jax.ShapeDtypeStruct((b, t, B_W), F32)],
        scratch_shapes=[pltpu.VMEM((8, SHIFT_W), F32)],
        compiler_params=_cparams(2),
        name="prep_swa",
    )(*args)


def _wkv_kernel(r_ref, k_ref, v_ref, na_ref, nb_ref, lw_ref, cs_ref, g_ref,
                lnw_ref, lnb_ref, rk_ref, ones_ref, o_ref, s_ref):
    c = WKV_CHUNK

    @pl.when(pl.program_id(0) == 0)
    def _():
        s_ref[...] = jnp.zeros_like(s_ref)

    gw = WKV_GROUP * HEAD
    lane = lax.broadcasted_iota(jnp.int32, (1, gw), 1)
    head_masks = [(lane >= h * HEAD) & (lane < (h + 1) * HEAD) for h in range(WKV_GROUP)]
    ti = lax.broadcasted_iota(jnp.int32, (c, gw), 0)
    tj = lax.broadcasted_iota(jnp.int32, (c, gw), 1) & (c - 1)
    strict = ti > tj
    incl = ti >= tj
    eye_t = jnp.where(ti == tj, 1.0, 0.0)
    li = lax.broadcasted_iota(jnp.int32, (gw, gw), 0)
    lj = lax.broadcasted_iota(jnp.int32, (gw, gw), 1)
    bdmask = (li // HEAD) == (lj // HEAD)
    ones_bd = ones_ref[...]

    def bd(x):
        xb = x.astype(BF16)
        zero = jnp.zeros_like(xb)
        return jnp.concatenate([jnp.where(m, xb, zero) for m in head_masks], axis=0)

    n_ck = r_ref.shape[1] // c
    n_state = s_ref.shape[0]
    chains = [(bi, slice(ck * c, (ck + 1) * c), slice(p * gw, (p + 1) * gw))
              for ck in range(n_ck) for bi in range(r_ref.shape[0]) for p in range(A_W // gw)]
    pairs = range(len(chains))
    rd = lambda ref, p: ref[chains[p]]
    rt, at_abs, bh, kh, wc, em, lhs0, rhs0 = [], [], [], [], [], [], [], []
    for p in pairs:
        cs = rd(cs_ref, p)
        mid = cs[c // 2 - 1:c // 2, :]
        last = cs[c - 1:c, :]
        e_neg = jnp.exp(mid - cs)
        rt_p = rd(r_ref, p) * jnp.exp(cs - mid)
        at_p = rd(na_ref, p) * jnp.exp(cs - rd(lw_ref, p) - mid)
        kt_p = rd(k_ref, p) * e_neg
        bt_p = rd(nb_ref, p) * e_neg
        em_p = jnp.exp(mid)
        e_end = jnp.exp(last - mid)
        rt.append(rt_p)
        at_abs.append(at_p * em_p)
        em.append(em_p)
        wc.append(jnp.exp(last))
        bh.append(bt_p * e_end)
        kh.append(kt_p * e_end)
        lhs0.append(jnp.concatenate([at_p, rt_p], axis=0))
        rhs0.append(jnp.concatenate([bd(bt_p), bd(kt_p)], axis=0))
    sc = [_bdot_nt(lhs0[p], rhs0[p]) for p in pairs]
    a_ab = [jnp.where(strict, sc[p][:c, :gw], 0.0) for p in pairs]
    a_rb = [jnp.where(incl, sc[p][c:, :gw], 0.0) for p in pairs]
    akrk = [jnp.concatenate([jnp.where(strict, sc[p][:c, gw:], 0.0),
                             jnp.where(incl, sc[p][c:, gw:], 0.0)], axis=0) for p in pairs]
    xv = [_bdot(akrk[p], bd(rd(v_ref, p))) for p in pairs]
    pinv = [eye_t + a_ab[p] for p in pairs]
    apow = [_bdot(a_ab[p], bd(a_ab[p])) for p in pairs]
    for _ in range(4):
        both = [_bdot(jnp.concatenate([apow[p], pinv[p]], axis=0), bd(apow[p])) for p in pairs]
        pinv = [pinv[p] + both[p][c:] for p in pairs]
        apow = [both[p][:c] for p in pairs]
    pinv = [pinv[p] + _bdot(pinv[p], bd(apow[p])) for p in pairs]
    ta = [_bdot(pinv[p], jnp.concatenate([bd(xv[p][:c]), bd(at_abs[p])], axis=1)) for p in pairs]
    uv = [ta[p][:, :gw] for p in pairs]
    ap = [ta[p][:, gw:] for p in pairs]
    rb = [_bdot(a_rb[p], jnp.concatenate([bd(ap[p]), bd(uv[p])], axis=1)) for p in pairs]
    rpap = [jnp.concatenate([rt[p] * em[p] + rb[p][:, :gw], ap[p]], axis=0) for p in pairs]
    yv = [rb[p][:, gw:] + xv[p][c:] for p in pairs]
    bk = [jnp.concatenate([bh[p], kh[p]], axis=0) for p in pairs]
    state = [s_ref[si] for si in range(n_state)]
    y = []
    for ck in range(n_ck):
        ps = [ck * n_state + si for si in range(n_state)]
        ru = [_bdot_nt(rpap[p], state[si]) for si, p in enumerate(ps)]
        y += [ru[si][:c] + yv[p] for si, p in enumerate(ps)]
        grams = [_bdot_tn(jnp.concatenate([ru[si][c:] + uv[p], rd(v_ref, p)], axis=0), bk[p])
                 for si, p in enumerate(ps)]
        state = [state[si] * wc[p] + jnp.where(bdmask, grams[si], 0.0) for si, p in enumerate(ps)]
    for si in range(n_state):
        s_ref[si] = state[si]
    stat_rows = []
    for p in pairs:
        stat_rows += [y[p], rd(r_ref, p) * rd(k_ref, p) * rk_ref[:, chains[p][2]]]
    stats_all = _bdot(jnp.concatenate(stat_rows, axis=0), ones_bd)
    stats = [stats_all[2 * c * p:2 * c * (p + 1)] for p in pairs]
    d = [y[p] - stats[p][:c] * (1.0 / HEAD) for p in pairs]
    var_all = _bdot(jnp.concatenate([d[p] * d[p] for p in pairs], axis=0), ones_bd) * (1.0 / HEAD)
    var = [var_all[c * p:c * (p + 1)] for p in pairs]
    for p in pairs:
        sl = chains[p][2]
        yn = d[p] * lax.rsqrt(var[p] + LN_X_EPS) * lnw_ref[:, sl] + lnb_ref[:, sl]
        out = yn + stats[p][c:] * rd(v_ref, p)
        o_ref[chains[p]] = out * _silu(rd(g_ref, p))


def _wkv(r, k, v, na, nb, lw, cs, gate, ln_w, ln_b, r_k):
    b, t, _ = r.shape
    c = WKV_CHUNK
    gw = WKV_GROUP * HEAD
    li = jnp.arange(gw)
    ones_bd = (li[:, None] // HEAD == li[None, :] // HEAD).astype(BF16)
    rows = WKV_STEP_CHUNKS * c
    assert t % rows == 0
    tile = pl.BlockSpec((b, rows, A_W), lambda i: (0, i, 0))
    vec = pl.BlockSpec((1, A_W), lambda i: (0, 0))
    return pl.pallas_call(
        _wkv_kernel,
        grid=(t // rows,),
        in_specs=[tile] * 8 + [vec] * 3 + [pl.BlockSpec((gw, gw), lambda i: (0, 0))],
        out_specs=tile,
        out_shape=jax.ShapeDtypeStruct((b, t, A_W), F32),
        scratch_shapes=[pltpu.VMEM((b * (A_W // gw), gw, gw), F32)],
        compiler_params=_cparams(1),
        name="wkv",
    )(r, k, v, na, nb, lw, cs, gate, ln_w.reshape(1, A_W), ln_b.reshape(1, A_W),
      r_k.reshape(1, A_W), ones_bd)


def _rope_angles_kernel(pos_ref, freq_ref, cos_ref, sin_ref, nsin_ref):
    ang = pos_ref[...].astype(F32) * freq_ref[...]
    sin = jnp.sin(ang)
    cos_ref[...] = jnp.cos(ang)
    sin_ref[...] = sin
    nsin_ref[...] = -sin


def _rope_tables(positions):
    b, t = positions.shape
    half = ROPE_DIM // 2
    n_rows = b * t * half // LANES
    inv_freq = jnp.power(jnp.float32(ROPE_THETA), -jnp.arange(half, dtype=F32) / half)
    pos_rep = jnp.repeat(positions.reshape(-1), half).reshape(n_rows, LANES)
    freq = jnp.tile(inv_freq, LANES // half).reshape(1, LANES)
    whole = pl.BlockSpec((n_rows, LANES), lambda i: (0, 0))
    cos_c, sin_c, nsin_c = pl.pallas_call(
        _rope_angles_kernel,
        grid=(1,),
        in_specs=[whole, pl.BlockSpec((1, LANES), lambda i: (0, 0))],
        out_specs=[whole] * 3,
        out_shape=[jax.ShapeDtypeStruct((n_rows, LANES), F32)] * 3,
        compiler_params=_cparams(1),
        name="rope_angles",
    )(pos_rep, freq)
    cos8, sin8, nsin8 = (z.reshape(b, t, half) for z in (cos_c, sin_c, nsin_c))
    rest = HEAD - ROPE_DIM
    cos_head = jnp.concatenate([cos8, cos8, jnp.ones((b, t, rest), F32)], axis=-1)
    sin_head = jnp.concatenate([nsin8, sin8, jnp.zeros((b, t, rest), F32)], axis=-1)
    n_heads = LANES // HEAD
    return jnp.tile(cos_head, (1, 1, n_heads)), jnp.tile(sin_head, (1, 1, n_heads))


def _rope(x, cos_t, sin_t, first8):
    fwd = pltpu.roll(x, LANES - ROPE_DIM // 2, axis=1)
    bwd = pltpu.roll(x, ROPE_DIM // 2, axis=1)
    return x * cos_t + jnp.where(first8, fwd, bwd) * sin_t


def _lane_tile(j):
    return slice(j * LANES, (j + 1) * LANES)


def _swa_consts():
    blk = WINDOW
    m0, m1 = _head_masks()
    lane = lax.broadcasted_iota(jnp.int32, (1, LANES), 1)
    first8 = (lane & (HEAD - 1)) < ROPE_DIM // 2
    ki = lax.broadcasted_iota(jnp.int32, (2 * blk, blk), 0)
    qi = lax.broadcasted_iota(jnp.int32, (2 * blk, blk), 1) + blk
    band = (ki <= qi) & (qi - ki < WINDOW)
    in_cur = ki >= blk
    return m0, m1, first8, band, in_cur


def _swa_scores(q_tile_at, k_win, v_win, cos_c, sin_c, consts):
    m0, m1, first8, _, _ = consts
    scale = 1.0 / math.sqrt(HEAD)
    values_t = [v_win[kt].T for kt in range(B_KVW // LANES)]
    scores_t = []
    for g in range(B_KVW // HEAD):
        k_tile = k_win[g // 2]
        swapped = pltpu.roll(k_tile, HEAD, axis=1)
        k_dup = jnp.where(m0, k_tile, swapped) if g % 2 == 0 else jnp.where(m0, swapped, k_tile)
        rows = []
        for pr in range(B_GROUP // 2):
            qp = _rope(q_tile_at(2 * g + pr), cos_c, sin_c, first8) * scale
            rows += [jnp.where(m0, qp, 0.0), jnp.where(m1, qp, 0.0)]
        scores_t.append(_bdot_nt(k_dup, jnp.concatenate(rows, axis=0)))
    return scores_t, values_t


def _swa_outputs(scores_t, values_t, has_prev, sink_ref, gate_tile_at, store_tile, consts):
    _, _, _, band, in_cur = consts
    blk = WINDOW
    valid = band & (has_prev | in_cur)
    ones_rows = jnp.ones((16, 2 * blk), BF16)
    for g in range(B_KVW // HEAD):
        v_t = values_t[g // 2][(g % 2) * HEAD:(g % 2 + 1) * HEAD]
        lhs = jnp.concatenate([v_t.astype(BF16), ones_rows], axis=0)
        for pr in range(B_GROUP // 2):
            qt = 2 * g + pr
            halves = []
            for half in range(2):
                c0 = (2 * pr + half) * blk
                s = jnp.where(valid, scores_t[g][:, c0:c0 + blk], -jnp.inf)
                sink = sink_ref[:, qt * LANES + half * HEAD:qt * LANES + half * HEAD + 1]
                m = jnp.maximum(jnp.max(s, axis=0, keepdims=True), sink)
                p = jnp.exp(s - m).astype(BF16)
                od = jnp.dot(lhs, p, preferred_element_type=F32)
                den = od[HEAD:HEAD + 1] + jnp.exp(sink - m)
                halves.append(od[:HEAD] * (1.0 / den))
            o_tile = jnp.concatenate(halves, axis=0).T
            store_tile(qt, o_tile * _silu(gate_tile_at(qt)))


def _out_proj_kernel(x_ref, ya_ref, yb_ref, w_ref, o_ref):
    y = jnp.concatenate([ya_ref[...].astype(BF16), yb_ref[...].astype(BF16)], axis=1)
    o_ref[...] = x_ref[...] + jnp.dot(y, w_ref[...], preferred_element_type=F32)


def _out_proj(x2d, ya, yb, w_bf16, tm):
    n, d = x2d.shape
    return pl.pallas_call(
        _out_proj_kernel,
        grid=(n // tm,),
        in_specs=[
            pl.BlockSpec((tm, d), lambda i: (i, 0)),
            pl.BlockSpec((tm, A_W), lambda i: (i, 0)),
            pl.BlockSpec((tm, B_W), lambda i: (i, 0)),
            pl.BlockSpec((A_W + B_W, d), lambda i: (0, 0), pipeline_mode=pl.Buffered(1)),
        ],
        out_specs=pl.BlockSpec((tm, d), lambda i: (i, 0)),
        out_shape=jax.ShapeDtypeStruct((n, d), F32),
        compiler_params=_cparams(1),
        name="out_proj",
    )(x2d, ya, yb, w_bf16)


def _odd_layer_kernel(x_ref, g_ref, win_ref, lnw_ref, lnb_ref, ws_ref, bst_ref, wout_ref,
                      fin_ref, o_ref, y_ref, *, final_norm):
    tm, d = x_ref.shape
    ch = SGU_CHUNK
    x = x_ref[...]
    ms = jnp.mean(x * x, axis=-1, keepdims=True)
    h = ((x * lax.rsqrt(ms + RMS_EPS)) * g_ref[...]).astype(BF16)
    v = jnp.dot(h, win_ref[:, d:2 * d], preferred_element_type=F32)
    mean = jnp.mean(v, axis=-1, keepdims=True)
    dv = v - mean
    var = jnp.mean(dv * dv, axis=-1, keepdims=True)
    vn = ((dv * lax.rsqrt(var + LN_EPS)) * lnw_ref[...] + lnb_ref[...]).astype(BF16)
    ti = lax.broadcasted_iota(jnp.int32, (ch, ch), 0)
    si = lax.broadcasted_iota(jnp.int32, (ch, ch), 1)
    causal = ti >= si
    cw = 2 * LANES
    zero_tile = jnp.zeros((ch, LANES), BF16)
    for j in range(d // cw):
        c0 = j * cw
        u = jnp.dot(h, win_ref[:, c0:c0 + cw], preferred_element_type=F32)
        gate = jnp.dot(h, win_ref[:, 2 * d + c0:2 * d + c0 + cw], preferred_element_type=F32)
        wm = jnp.concatenate(
            [jnp.where(causal, ws_ref[2 * j + gi], 0.0).astype(BF16) for gi in range(2)], axis=1)
        bias = jnp.concatenate(
            [jnp.broadcast_to(bst_ref[:, 2 * j + gi:2 * j + gi + 1], (ch, LANES)) for gi in range(2)],
            axis=1)
        mixed = []
        for ci in range(tm // ch):
            vc = vn[ci * ch:(ci + 1) * ch, c0:c0 + cw]
            v_bd = jnp.concatenate(
                [jnp.concatenate([vc[:, :LANES], zero_tile], axis=1),
                 jnp.concatenate([zero_tile, vc[:, LANES:]], axis=1)], axis=0)
            mixed.append(jnp.dot(wm, v_bd, preferred_element_type=F32) + bias)
        y_ref[:, c0:c0 + cw] = (u * jnp.concatenate(mixed, axis=0) * _silu(gate)).astype(BF16)
    out = x + jnp.dot(y_ref[...], wout_ref[...], preferred_element_type=F32)
    if final_norm:
        ms = jnp.mean(out * out, axis=-1, keepdims=True)
        out = (out * lax.rsqrt(ms + RMS_EPS)) * fin_ref[...]
    o_ref[...] = out


def _odd_layer(x2d, g, w_in_bf16, ln_w, ln_b, ws, bs, w_out_bf16, final_g, tm):
    n, d = x2d.shape
    final_norm = final_g is not None
    fin = final_g if final_norm else jnp.ones((d,), F32)
    row_tile = pl.BlockSpec((tm, d), lambda i: (i, 0))
    vec = pl.BlockSpec((1, d), lambda i: (0, 0))
    return pl.pallas_call(
        functools.partial(_odd_layer_kernel, final_norm=final_norm),
        grid=(n // tm,),
        in_specs=[
            row_tile, vec,
            pl.BlockSpec((d, 3 * d), lambda i: (0, 0), pipeline_mode=pl.Buffered(1)),
            vec, vec,
            pl.BlockSpec((SGU_GROUPS, SGU_CHUNK, SGU_CHUNK), lambda i: (0, 0, 0)),
            pl.BlockSpec((SGU_CHUNK, SGU_GROUPS), lambda i: (0, 0)),
            pl.BlockSpec((d, d), lambda i: (0, 0), pipeline_mode=pl.Buffered(1)),
            vec,
        ],
        out_specs=row_tile,
        out_shape=jax.ShapeDtypeStruct((n, d), F32),
        scratch_shapes=[pltpu.VMEM((tm, d), BF16)],
        compiler_params=_cparams(1),
        name="odd_layer",
    )(x2d, g.reshape(1, d), w_in_bf16, ln_w.reshape(1, d), ln_b.reshape(1, d), ws, bs.T,
      w_out_bf16, fin.reshape(1, d))


def kernel(x, positions, e_norm, e_w_in, e_mu, rwkv_w0, rwkv_w2, rwkv_a0, rwkv_a2, rwkv_k_k, rwkv_k_a, rwkv_r_k, rwkv_ln_w, rwkv_ln_b, rwkv_v0, rwkv_v1, rwkv_v2, attn_sinks, e_w_out, o_norm, o_w_in, sgu_ln_w, sgu_ln_b, sgu_ws, sgu_bs, o_w_out, final_norm):
    b, t, d = x.shape
    n = b * t
    depth = e_norm.shape[0] + o_norm.shape[0]
    assert t % 256 == 0 and d == D_MODEL and depth % 2 == 0
    x2d = x.reshape(n, d)
    cos_t, sin_t = _rope_tables(positions)
    v_first = None
    for layer in range(depth):
        if layer % 2 == 0:
            e = layer // 2
            w_a = e_w_in[e, :, :SHIFT_W].astype(BF16)
            w_rest = e_w_in[e, :, SHIFT_W:].astype(BF16)
            r3 = lambda z: z.reshape(b, t, z.shape[-1])
            vres = None if e == 0 else (v_first, rwkv_v0[e - 1], rwkv_v1[e - 1], rwkv_v2[e - 1])
            a_gate, q, kv, b_gate = _norm_matmul(
                x2d, e_norm[e], w_rest, (A_W, B_W, 2 * B_KVW, B_W), tm=512)
            r, k2, v, na, nb, lw, cs, yb = _prep_swa(
                r3(x2d), e_norm[e], w_a, e_mu[e], rwkv_w0[e], rwkv_w2[e], rwkv_a0[e],
                rwkv_a2[e], rwkv_k_k[e], rwkv_k_a[e], vres, r3(q), r3(kv), cos_t, sin_t, r3(b_gate),
                attn_sinks[e], tt=256)
            if e == 0:
                v_first = v
            ya = _wkv(r, k2, v, na, nb, lw, cs, r3(a_gate), rwkv_ln_w[e], rwkv_ln_b[e], rwkv_r_k[e])
            x2d = _out_proj(x2d, ya.reshape(n, A_W), yb.reshape(n, B_W), e_w_out[e].astype(BF16), tm=512)
        else:
            o = layer // 2
            fin = final_norm if layer == depth - 1 else None
            x2d = _odd_layer(x2d, o_norm[o], o_w_in[o].astype(BF16), sgu_ln_w[o], sgu_ln_b[o],
                             sgu_ws[o], sgu_bs[o], o_w_out[o].astype(BF16), fin, tm=256)
    return x2d.reshape(b, t, d)
```

```python
import functools
import math

import jax
import jax.numpy as jnp
from jax import lax
from jax.experimental import pallas as pl
from jax.experimental.pallas import tpu as pltpu

F32 = jnp.float32
BF16 = jnp.bfloat16

D_MODEL = 2048
HEAD = 64
A_W = 1024
B_W = 1024
B_KVW = 256
B_GROUP = 4
LORA = 64
SHIFT_W = 3 * A_W + 2 * LORA
WINDOW = 128
ROPE_DIM = 16
ROPE_THETA = 500000.0
SGU_CHUNK = 128
SGU_GROUPS = 16
RMS_EPS = 1e-5
LN_EPS = 1e-5
LN_X_EPS = HEAD * 1e-5

LANES = 128
WKV_CHUNK = 64
WKV_GROUP = 4
WKV_STEP_CHUNKS = 2
VMEM_LIMIT = 56 * 1024 * 1024


def _cparams(n_axes):
    return pltpu.CompilerParams(
        dimension_semantics=("arbitrary",) * n_axes, vmem_limit_bytes=VMEM_LIMIT)


def _bdot(a, b):
    return jnp.dot(a.astype(BF16), b.astype(BF16), preferred_element_type=F32)


def _bdot_nt(a, b):
    return lax.dot_general(a.astype(BF16), b.astype(BF16), (((1,), (1,)), ((), ())),
                           preferred_element_type=F32)


def _bdot_tn(a, b):
    return lax.dot_general(a.astype(BF16), b.astype(BF16), (((0,), (0,)), ((), ())),
                           preferred_element_type=F32)


def _split_dot(x, ones_bf16, passes):
    acc = None
    rem = x
    for _ in range(passes):
        piece = rem.astype(BF16)
        term = jnp.dot(piece, ones_bf16, preferred_element_type=F32)
        acc = term if acc is None else acc + term
        rem = rem - piece.astype(F32)
    return acc


def _silu(x):
    return x * (1.0 / (1.0 + jnp.exp(-x)))


def _sigmoid(x):
    return 1.0 / (1.0 + jnp.exp(-x))


def _head_masks():
    lane = lax.broadcasted_iota(jnp.int32, (1, LANES), 1)
    return lane < HEAD, lane >= HEAD


def _norm_matmul_kernel(x_ref, g_ref, w_ref, *out_refs, seg_widths, col_chunk):
    x = x_ref[...]
    ms = jnp.mean(x * x, axis=-1, keepdims=True)
    h = ((x * lax.rsqrt(ms + RMS_EPS)) * g_ref[...]).astype(BF16)
    col = 0
    for o_ref, width in zip(out_refs, seg_widths):
        for c0 in range(0, width, col_chunk):
            cw = min(col_chunk, width - c0)
            o_ref[:, c0:c0 + cw] = jnp.dot(
                h, w_ref[:, col + c0:col + c0 + cw], preferred_element_type=F32)
        col += width


def _norm_matmul(x2d, g, w_bf16, seg_widths, tm):
    n, d = x2d.shape
    ncol = w_bf16.shape[1]
    assert sum(seg_widths) == ncol and n % tm == 0
    kern = functools.partial(_norm_matmul_kernel, seg_widths=tuple(seg_widths), col_chunk=512)
    return pl.pallas_call(
        kern,
        grid=(n // tm,),
        in_specs=[
            pl.BlockSpec((tm, d), lambda i: (i, 0)),
            pl.BlockSpec((1, d), lambda i: (0, 0)),
            pl.BlockSpec((d, ncol), lambda i: (0, 0), pipeline_mode=pl.Buffered(1)),
        ],
        out_specs=[pl.BlockSpec((tm, w), lambda i: (i, 0)) for w in seg_widths],
        out_shape=[jax.ShapeDtypeStruct((n, w), F32) for w in seg_widths],
        compiler_params=_cparams(1),
        name="norm_matmul",
    )(x2d, g.reshape(1, d), w_bf16)


def _prep_swa_kernel(*refs, has_vres):
    (x_ref, g_ref, w_ref, mu_ref, w0_ref, a0_ref, wa2_ref, kk_ref, ka_ref, tri_ref, ones_ref) = refs[:11]
    if has_vres:
        vf_ref, v0_ref, v1_ref, v2_ref = refs[11:15]
    (q_ref, kvc_ref, kvp_ref, cosc_ref, sinc_ref, cosp_ref, sinp_ref, bg_ref, sink_ref,
     r_out, k_out, v_out, na_out, nb_out, lw_out, cs_out, yb_out, carry_ref) = refs[-18:]

    @pl.when(pl.program_id(1) == 0)
    def _():
        carry_ref[...] = jnp.zeros_like(carry_ref)

    x = x_ref[0]
    tt = x.shape[0]

    consts = _swa_consts()
    first8 = consts[2]
    blk = WINDOW
    n_blk = tt // blk
    k_roped, v_rows = [], []
    for j in range(n_blk + 1):
        if j == 0:
            kv, cos_j, sin_j = kvp_ref[0], cosp_ref[0], sinp_ref[0]
        else:
            rows = slice((j - 1) * blk, j * blk)
            kv, cos_j, sin_j = kvc_ref[0, rows], cosc_ref[0, rows], sinc_ref[0, rows]
        k_roped.append([_rope(kv[:, _lane_tile(kt)], cos_j, sin_j, first8)
                        for kt in range(B_KVW // LANES)])
        v_rows.append([kv[:, B_KVW + kt * LANES:B_KVW + (kt + 1) * LANES]
                       for kt in range(B_KVW // LANES)])
    attn = []
    for j in range(n_blk):
        rows = slice(j * blk, (j + 1) * blk)
        k_win = [jnp.concatenate([k_roped[j][kt], k_roped[j + 1][kt]], axis=0)
                 for kt in range(B_KVW // LANES)]
        v_win = [jnp.concatenate([v_rows[j][kt], v_rows[j + 1][kt]], axis=0)
                 for kt in range(B_KVW // LANES)]
        attn.append(_swa_scores(lambda qt, rows=rows: q_ref[0, rows, _lane_tile(qt)], k_win, v_win,
                                cosc_ref[0, rows], sinc_ref[0, rows], consts))

    ms = jnp.mean(x * x, axis=-1, keepdims=True)
    h = ((x * lax.rsqrt(ms + RMS_EPS)) * g_ref[...]).astype(BF16)
    row = lax.broadcasted_iota(jnp.int32, (tt, 1), 0)

    def shifted(c0, width):
        z = jnp.dot(h, w_ref[:, c0:c0 + width], preferred_element_type=F32)
        prev = jnp.where(row == 0, carry_ref[0:1, c0:c0 + width], pltpu.roll(z, 1, axis=0))
        carry_ref[0:1, c0:c0 + width] = z[tt - 1:tt, :]
        return z + (prev - z) * mu_ref[:, c0:c0 + width]

    xwa = shifted(3 * A_W, 2 * LORA)
    lane = lax.broadcasted_iota(jnp.int32, (1, 2 * LORA), 1)
    lora_in = jnp.where(lane < LORA, jnp.tanh(xwa), xwa)
    lora = _bdot(lora_in, wa2_ref[...])
    wpre = w0_ref[...] + lora[:, 0:A_W]
    neg = -wpre
    softplus = jnp.maximum(neg, 0.0) + jnp.log(1.0 + jnp.exp(-jnp.abs(neg)))
    w_log = -softplus - 0.5
    lw = -jnp.exp(w_log)
    a = _sigmoid(a0_ref[...] + lora[:, A_W:2 * A_W])
    lw_out[0] = lw
    v = shifted(2 * A_W, A_W)
    if has_vres:
        vf = vf_ref[0]
        gate = _sigmoid(v0_ref[...] + _bdot(_bdot(v, v1_ref[...]), v2_ref[...]))
        v = v + (vf - v) * gate
    v_out[0] = v
    r_out[0] = shifted(0, A_W)
    hi = lw.astype(BF16)
    lo = (lw - hi.astype(F32)).astype(BF16)
    tri = tri_ref[...]
    cs_out[0] = (jnp.dot(tri, hi, preferred_element_type=F32)
                 + jnp.dot(tri, lo, preferred_element_type=F32))
    k = shifted(A_W, A_W)
    k_out[0] = k * (1.0 + (a - 1.0) * ka_ref[...])
    kk = k * kk_ref[...]
    ones_bd = ones_ref[...]
    gw = ones_bd.shape[0]
    for p in range(A_W // gw):
        sl = slice(p * gw, (p + 1) * gw)
        kkp = kk[:, sl]
        ss = _bdot(kkp * kkp, ones_bd)
        kkn = kkp * lax.rsqrt(ss + 1e-12)
        na_out[0, :, sl] = -kkn
        nb_out[0, :, sl] = kkn * a[:, sl]

    for j in range(n_blk):
        rows = slice(j * blk, (j + 1) * blk)
        has_prev = (pl.program_id(1) > 0) if j == 0 else True

        def store_tile(qt, val, rows=rows):
            yb_out[0, rows, _lane_tile(qt)] = val

        _swa_outputs(attn[j][0], attn[j][1], has_prev, sink_ref,
                     lambda qt, rows=rows: bg_ref[0, rows, _lane_tile(qt)], store_tile, consts)


def _prep_swa(x3d, g, w_in_stack, layer, mu, w0, w2, a0, a2, k_k, k_a, vres, q, kv, cos_t, sin_t,
              b_gate, sinks, tt):
    b, t, d = x3d.shape
    has_vres = vres is not None
    wa2 = jnp.zeros((2 * LORA, 2 * A_W), F32)
    wa2 = wa2.at[:LORA, :A_W].set(w2).at[LORA:, A_W:].set(a2).astype(BF16)
    ti = jnp.arange(tt)
    tri = ((ti[:, None] >= ti[None, :]) &
           (ti[:, None] // WKV_CHUNK == ti[None, :] // WKV_CHUNK)).astype(BF16)
    gw = WKV_GROUP * HEAD
    li = jnp.arange(gw)
    ones_bd = (li[:, None] // HEAD == li[None, :] // HEAD).astype(BF16)
    row = lambda p: p.reshape(1, -1)
    full = lambda shape: pl.BlockSpec(shape, lambda bi, i: (0,) * len(shape))
    tile = lambda w: pl.BlockSpec((1, tt, w), lambda bi, i: (bi, i, 0))
    in_specs = [
        tile(d), full((1, d)),
        pl.BlockSpec((None, d, SHIFT_W), lambda bi, i: (layer, 0, 0), pipeline_mode=pl.Buffered(1)),
        full((1, SHIFT_W)), full((1, A_W)), full((1, A_W)), full((2 * LORA, 2 * A_W)),
        full((1, A_W)), full((1, A_W)), full((tt, tt)), full((gw, gw)),
    ]
    args = [x3d, row(g), w_in_stack, row(mu), row(w0), row(a0), wa2, row(k_k), row(k_a), tri, ones_bd]
    if has_vres:
        v_first, v0, v1, v2 = vres
        in_specs += [tile(A_W), full((1, A_W)), full(v1.shape), full(v2.shape)]
        args += [v_first, row(v0), v1.astype(BF16), v2.astype(BF16)]
    blk = WINDOW
    prev_blk = lambda w: pl.BlockSpec(
        (1, blk, w), lambda bi, i: (bi, jnp.maximum(i * (tt // blk) - 1, 0), 0))
    in_specs += [tile(B_W), tile(2 * B_KVW), prev_blk(2 * B_KVW), tile(LANES), tile(LANES),
                 prev_blk(LANES), prev_blk(LANES), tile(B_W), full((1, B_W))]
    args += [q, kv, kv, cos_t, sin_t, cos_t, sin_t, b_gate,
             jnp.repeat(sinks.astype(F32), HEAD).reshape(1, B_W)]
    return pl.pallas_call(
        functools.partial(_prep_swa_kernel, has_vres=has_vres),
        grid=(b, t // tt),
        in_specs=in_specs,
        out_specs=[tile(A_W)] * 7 + [tile(B_W)],
        out_shape=[jax.ShapeDtypeStruct((b, t, A_W), F32)] * 7 + [jax.ShapeDtypeStruct((b, t, B_W), F32)],
        scratch_shapes=[pltpu.VMEM((8, SHIFT_W), F32)],
        compiler_params=_cparams(2),
        name="prep_swa",
    )(*args)


def _wkv_kernel(r_ref, k_ref, v_ref, na_ref, nb_ref, lw_ref, cs_ref, g_ref,
                lnw_ref, lnb_ref, rk_ref, ones_ref, o_ref, s_ref):
    c = WKV_CHUNK

    @pl.when(pl.program_id(0) == 0)
    def _():
        s_ref[...] = jnp.zeros_like(s_ref)

    gw = WKV_GROUP * HEAD
    lane = lax.broadcasted_iota(jnp.int32, (1, gw), 1)
    head_masks = [(lane >= h * HEAD) & (lane < (h + 1) * HEAD) for h in range(WKV_GROUP)]
    ti = lax.broadcasted_iota(jnp.int32, (c, gw), 0)
    tj = lax.broadcasted_iota(jnp.int32, (c, gw), 1) & (c - 1)
    strict = ti > tj
    incl = ti >= tj
    eye_t = jnp.where(ti == tj, 1.0, 0.0)
    li = lax.broadcasted_iota(jnp.int32, (gw, gw), 0)
    lj = lax.broadcasted_iota(jnp.int32, (gw, gw), 1)
    bdmask = (li // HEAD) == (lj // HEAD)
    ones_bd = ones_ref[...]

    def bd(x):
        xb = x.astype(BF16)
        zero = jnp.zeros_like(xb)
        return jnp.concatenate([jnp.where(m, xb, zero) for m in head_masks], axis=0)

    n_ck = r_ref.shape[1] // c
    n_state = s_ref.shape[0]
    chains = [(bi, slice(ck * c, (ck + 1) * c), slice(p * gw, (p + 1) * gw))
              for ck in range(n_ck) for bi in range(r_ref.shape[0]) for p in range(A_W // gw)]
    pairs = range(len(chains))
    rd = lambda ref, p: ref[chains[p]]
    rt, at_abs, bh, kh, wc, em, lhs0, rhs0 = [], [], [], [], [], [], [], []
    for p in pairs:
        cs = rd(cs_ref, p)
        mid = cs[c // 2 - 1:c // 2, :]
        last = cs[c - 1:c, :]
        e_neg = jnp.exp(mid - cs)
        rt_p = rd(r_ref, p) * jnp.exp(cs - mid)
        at_p = rd(na_ref, p) * jnp.exp(cs - rd(lw_ref, p) - mid)
        kt_p = rd(k_ref, p) * e_neg
        bt_p = rd(nb_ref, p) * e_neg
        em_p = jnp.exp(mid)
        e_end = jnp.exp(last - mid)
        rt.append(rt_p)
        at_abs.append(at_p * em_p)
        em.append(em_p)
        wc.append(jnp.exp(last))
        bh.append(bt_p * e_end)
        kh.append(kt_p * e_end)
        lhs0.append(jnp.concatenate([at_p, rt_p], axis=0))
        rhs0.append(jnp.concatenate([bd(bt_p), bd(kt_p)], axis=0))
    sc = [_bdot_nt(lhs0[p], rhs0[p]) for p in pairs]
    a_ab = [jnp.where(strict, sc[p][:c, :gw], 0.0) for p in pairs]
    a_rb = [jnp.where(incl, sc[p][c:, :gw], 0.0) for p in pairs]
    akrk = [jnp.concatenate([jnp.where(strict, sc[p][:c, gw:], 0.0),
                             jnp.where(incl, sc[p][c:, gw:], 0.0)], axis=0) for p in pairs]
    xv = [_bdot(akrk[p], bd(rd(v_ref, p))) for p in pairs]
    pinv = [eye_t + a_ab[p] for p in pairs]
    apow = [_bdot(a_ab[p], bd(a_ab[p])) for p in pairs]
    for _ in range(4):
        both = [_bdot(jnp.concatenate([apow[p], pinv[p]], axis=0), bd(apow[p])) for p in pairs]
        pinv = [pinv[p] + both[p][c:] for p in pairs]
        apow = [both[p][:c] for p in pairs]
    pinv = [pinv[p] + _bdot(pinv[p], bd(apow[p])) for p in pairs]
    ta = [_bdot(pinv[p], jnp.concatenate([bd(xv[p][:c]), bd(at_abs[p])], axis=1)) for p in pairs]
    uv = [ta[p][:, :gw] for p in pairs]
    ap = [ta[p][:, gw:] for p in pairs]
    rb = [_bdot(a_rb[p], jnp.concatenate([bd(ap[p]), bd(uv[p])], axis=1)) for p in pairs]
    rpap = [jnp.concatenate([rt[p] * em[p] + rb[p][:, :gw], ap[p]], axis=0) for p in pairs]
    yv = [rb[p][:, gw:] + xv[p][c:] for p in pairs]
    bk = [jnp.concatenate([bh[p], kh[p]], axis=0) for p in pairs]
    state = [s_ref[si] for si in range(n_state)]
    y = []
    for ck in range(n_ck):
        ps = [ck * n_state + si for si in range(n_state)]
        ru = [_bdot_nt(rpap[p], state[si]) for si, p in enumerate(ps)]
        y += [ru[si][:c] + yv[p] for si, p in enumerate(ps)]
        grams = [_bdot_tn(jnp.concatenate([ru[si][c:] + uv[p], rd(v_ref, p)], axis=0), bk[p])
                 for si, p in enumerate(ps)]
        state = [state[si] * wc[p] + jnp.where(bdmask, grams[si], 0.0) for si, p in enumerate(ps)]
    for si in range(n_state):
        s_ref[si] = state[si]
    stat_rows = []
    for p in pairs:
        stat_rows += [y[p], rd(r_ref, p) * rd(k_ref, p) * rk_ref[:, chains[p][2]]]
    stats_all = _bdot(jnp.concatenate(stat_rows, axis=0), ones_bd)
    stats = [stats_all[2 * c * p:2 * c * (p + 1)] for p in pairs]
    d = [y[p] - stats[p][:c] * (1.0 / HEAD) for p in pairs]
    var_all = _bdot(jnp.concatenate([d[p] * d[p] for p in pairs], axis=0), ones_bd) * (1.0 / HEAD)
    var = [var_all[c * p:c * (p + 1)] for p in pairs]
    for p in pairs:
        sl = chains[p][2]
        yn = d[p] * lax.rsqrt(var[p] + LN_X_EPS) * lnw_ref[:, sl] + lnb_ref[:, sl]
        out = yn + stats[p][c:] * rd(v_ref, p)
        o_ref[chains[p]] = out * _silu(rd(g_ref, p))


def _wkv(r, k, v, na, nb, lw, cs, gate, ln_w, ln_b, r_k):
    b, t, _ = r.shape
    c = WKV_CHUNK
    gw = WKV_GROUP * HEAD
    li = jnp.arange(gw)
    ones_bd = (li[:, None] // HEAD == li[None, :] // HEAD).astype(BF16)
    rows = WKV_STEP_CHUNKS * c
    assert t % rows == 0
    tile = pl.BlockSpec((b, rows, A_W), lambda i: (0, i, 0))
    vec = pl.BlockSpec((1, A_W), lambda i: (0, 0))
    return pl.pallas_call(
        _wkv_kernel,
        grid=(t // rows,),
        in_specs=[tile] * 8 + [vec] * 3 + [pl.BlockSpec((gw, gw), lambda i: (0, 0))],
        out_specs=tile,
        out_shape=jax.ShapeDtypeStruct((b, t, A_W), F32),
        scratch_shapes=[pltpu.VMEM((b * (A_W // gw), gw, gw), F32)],
        compiler_params=_cparams(1),
        name="wkv",
    )(r, k, v, na, nb, lw, cs, gate, ln_w.reshape(1, A_W), ln_b.reshape(1, A_W),
      r_k.reshape(1, A_W), ones_bd)


def _rope_angles_kernel(pos_ref, freq_ref, cos_ref, sin_ref, nsin_ref):
    ang = pos_ref[...].astype(F32) * freq_ref[...]
    sin = jnp.sin(ang)
    cos_ref[...] = jnp.cos(ang)
    sin_ref[...] = sin
    nsin_ref[...] = -sin


def _rope_tables(positions):
    b, t = positions.shape
    half = ROPE_DIM // 2
    n_rows = b * t * half // LANES
    inv_freq = jnp.power(jnp.float32(ROPE_THETA), -jnp.arange(half, dtype=F32) / half)
    pos_rep = jnp.repeat(positions.reshape(-1), half).reshape(n_rows, LANES)
    freq = jnp.tile(inv_freq, LANES // half).reshape(1, LANES)
    whole = pl.BlockSpec((n_rows, LANES), lambda i: (0, 0))
    cos_c, sin_c, nsin_c = pl.pallas_call(
        _rope_angles_kernel,
        grid=(1,),
        in_specs=[whole, pl.BlockSpec((1, LANES), lambda i: (0, 0))],
        out_specs=[whole] * 3,
        out_shape=[jax.ShapeDtypeStruct((n_rows, LANES), F32)] * 3,
        compiler_params=_cparams(1),
        name="rope_angles",
    )(pos_rep, freq)
    cos8, sin8, nsin8 = (z.reshape(b, t, half) for z in (cos_c, sin_c, nsin_c))
    rest = HEAD - ROPE_DIM
    cos_head = jnp.concatenate([cos8, cos8, jnp.ones((b, t, rest), F32)], axis=-1)
    sin_head = jnp.concatenate([nsin8, sin8, jnp.zeros((b, t, rest), F32)], axis=-1)
    n_heads = LANES // HEAD
    return jnp.tile(cos_head, (1, 1, n_heads)), jnp.tile(sin_head, (1, 1, n_heads))


def _rope(x, cos_t, sin_t, first8):
    fwd = pltpu.roll(x, LANES - ROPE_DIM // 2, axis=1)
    bwd = pltpu.roll(x, ROPE_DIM // 2, axis=1)
    return x * cos_t + jnp.where(first8, fwd, bwd) * sin_t


def _lane_tile(j):
    return slice(j * LANES, (j + 1) * LANES)


def _swa_consts():
    blk = WINDOW
    m0, m1 = _head_masks()
    lane = lax.broadcasted_iota(jnp.int32, (1, LANES), 1)
    first8 = (lane & (HEAD - 1)) < ROPE_DIM // 2
    ki = lax.broadcasted_iota(jnp.int32, (2 * blk, blk), 0)
    qi = lax.broadcasted_iota(jnp.int32, (2 * blk, blk), 1) + blk
    band = (ki <= qi) & (qi - ki < WINDOW)
    in_cur = ki >= blk
    return m0, m1, first8, band, in_cur


def _swa_scores(q_tile_at, k_win, v_win, cos_c, sin_c, consts):
    m0, m1, first8, _, _ = consts
    scale = 1.0 / math.sqrt(HEAD)
    values_t = [v_win[kt].T for kt in range(B_KVW // LANES)]
    scores_t = []
    for g in range(B_KVW // HEAD):
        k_tile = k_win[g // 2]
        swapped = pltpu.roll(k_tile, HEAD, axis=1)
        k_dup = jnp.where(m0, k_tile, swapped) if g % 2 == 0 else jnp.where(m0, swapped, k_tile)
        rows = []
        for pr in range(B_GROUP // 2):
            qp = _rope(q_tile_at(2 * g + pr), cos_c, sin_c, first8) * scale
            rows += [jnp.where(m0, qp, 0.0), jnp.where(m1, qp, 0.0)]
        scores_t.append(_bdot_nt(k_dup, jnp.concatenate(rows, axis=0)))
    return scores_t, values_t


def _swa_outputs(scores_t, values_t, has_prev, sink_ref, gate_tile_at, store_tile, consts):
    _, _, _, band, in_cur = consts
    blk = WINDOW
    valid = band & (has_prev | in_cur)
    ones_rows = jnp.ones((16, 2 * blk), BF16)
    for g in range(B_KVW // HEAD):
        v_t = values_t[g // 2][(g % 2) * HEAD:(g % 2 + 1) * HEAD]
        lhs = jnp.concatenate([v_t.astype(BF16), ones_rows], axis=0)
        for pr in range(B_GROUP // 2):
            qt = 2 * g + pr
            halves = []
            for half in range(2):
                c0 = (2 * pr + half) * blk
                s = jnp.where(valid, scores_t[g][:, c0:c0 + blk], -jnp.inf)
                sink = sink_ref[:, qt * LANES + half * HEAD:qt * LANES + half * HEAD + 1]
                m = jnp.maximum(jnp.max(s, axis=0, keepdims=True), sink)
                p = jnp.exp(s - m).astype(BF16)
                od = jnp.dot(lhs, p, preferred_element_type=F32)
                den = od[HEAD:HEAD + 1] + jnp.exp(sink - m)
                halves.append(od[:HEAD] * (1.0 / den))
            o_tile = jnp.concatenate(halves, axis=0).T
            store_tile(qt, o_tile * _silu(gate_tile_at(qt)))


def _out_proj_kernel(x_ref, ya_ref, yb_ref, w_ref, o_ref):
    y = jnp.concatenate([ya_ref[...].astype(BF16), yb_ref[...].astype(BF16)], axis=1)
    o_ref[...] = x_ref[...] + jnp.dot(y, w_ref[...], preferred_element_type=F32)


def _out_proj(x2d, ya, yb, w_stack, layer, tm):
    n, d = x2d.shape
    return pl.pallas_call(
        _out_proj_kernel,
        grid=(n // tm,),
        in_specs=[
            pl.BlockSpec((tm, d), lambda i: (i, 0)),
            pl.BlockSpec((tm, A_W), lambda i: (i, 0)),
            pl.BlockSpec((tm, B_W), lambda i: (i, 0)),
            pl.BlockSpec((None, A_W + B_W, d), lambda i: (layer, 0, 0), pipeline_mode=pl.Buffered(1)),
        ],
        out_specs=pl.BlockSpec((tm, d), lambda i: (i, 0)),
        out_shape=jax.ShapeDtypeStruct((n, d), F32),
        compiler_params=_cparams(1),
        name="out_proj",
    )(x2d, ya, yb, w_stack)


def _odd_layer_kernel(x_ref, g_ref, win_ref, lnw_ref, lnb_ref, ws_ref, bst_ref, wout_ref,
                      fin_ref, o_ref, y_ref, *, final_norm):
    tm, d = x_ref.shape
    ch = SGU_CHUNK
    x = x_ref[...]
    ms = jnp.mean(x * x, axis=-1, keepdims=True)
    h = ((x * lax.rsqrt(ms + RMS_EPS)) * g_ref[...]).astype(BF16)
    v = jnp.dot(h, win_ref[:, d:2 * d], preferred_element_type=F32)
    mean = jnp.mean(v, axis=-1, keepdims=True)
    dv = v - mean
    var = jnp.mean(dv * dv, axis=-1, keepdims=True)
    vn = ((dv * lax.rsqrt(var + LN_EPS)) * lnw_ref[...] + lnb_ref[...]).astype(BF16)
    ti = lax.broadcasted_iota(jnp.int32, (ch, ch), 0)
    si = lax.broadcasted_iota(jnp.int32, (ch, ch), 1)
    causal = ti >= si
    cw = 2 * LANES
    zero_tile = jnp.zeros((ch, LANES), BF16)
    for j in range(d // cw):
        c0 = j * cw
        u = jnp.dot(h, win_ref[:, c0:c0 + cw], preferred_element_type=F32)
        gate = jnp.dot(h, win_ref[:, 2 * d + c0:2 * d + c0 + cw], preferred_element_type=F32)
        wm = jnp.concatenate(
            [jnp.where(causal, ws_ref[2 * j + gi], 0.0).astype(BF16) for gi in range(2)], axis=1)
        bias = jnp.concatenate(
            [jnp.broadcast_to(bst_ref[:, 2 * j + gi:2 * j + gi + 1], (ch, LANES)) for gi in range(2)],
            axis=1)
        mixed = []
        for ci in range(tm // ch):
            vc = vn[ci * ch:(ci + 1) * ch, c0:c0 + cw]
            v_bd = jnp.concatenate(
                [jnp.concatenate([vc[:, :LANES], zero_tile], axis=1),
                 jnp.concatenate([zero_tile, vc[:, LANES:]], axis=1)], axis=0)
            mixed.append(jnp.dot(wm, v_bd, preferred_element_type=F32) + bias)
        y_ref[:, c0:c0 + cw] = (u * jnp.concatenate(mixed, axis=0) * _silu(gate)).astype(BF16)
    out = x + jnp.dot(y_ref[...], wout_ref[...], preferred_element_type=F32)
    if final_norm:
        ms = jnp.mean(out * out, axis=-1, keepdims=True)
        out = (out * lax.rsqrt(ms + RMS_EPS)) * fin_ref[...]
    o_ref[...] = out


def _odd_layer(x2d, g, w_in_stack, ln_w, ln_b, ws, bs, w_out_stack, layer, final_g, tm):
    n, d = x2d.shape
    final_norm = final_g is not None
    fin = final_g if final_norm else jnp.ones((d,), F32)
    row_tile = pl.BlockSpec((tm, d), lambda i: (i, 0))
    vec = pl.BlockSpec((1, d), lambda i: (0, 0))
    return pl.pallas_call(
        functools.partial(_odd_layer_kernel, final_norm=final_norm),
        grid=(n // tm,),
        in_specs=[
            row_tile, vec,
            pl.BlockSpec((None, d, 3 * d), lambda i: (layer, 0, 0), pipeline_mode=pl.Buffered(1)),
            vec, vec,
            pl.BlockSpec((SGU_GROUPS, SGU_CHUNK, SGU_CHUNK), lambda i: (0, 0, 0)),
            pl.BlockSpec((SGU_CHUNK, SGU_GROUPS), lambda i: (0, 0)),
            pl.BlockSpec((None, d, d), lambda i: (layer, 0, 0), pipeline_mode=pl.Buffered(1)),
            vec,
        ],
        out_specs=row_tile,
        out_shape=jax.ShapeDtypeStruct((n, d), F32),
        scratch_shapes=[pltpu.VMEM((tm, d), BF16)],
        compiler_params=_cparams(1),
        name="odd_layer",
    )(x2d, g.reshape(1, d), w_in_stack, ln_w.reshape(1, d), ln_b.reshape(1, d), ws, bs.T,
      w_out_stack, fin.reshape(1, d))


def kernel(x, positions, e_norm, e_w_in, e_mu, rwkv_w0, rwkv_w2, rwkv_a0, rwkv_a2, rwkv_k_k, rwkv_k_a, rwkv_r_k, rwkv_ln_w, rwkv_ln_b, rwkv_v0, rwkv_v1, rwkv_v2, attn_sinks, e_w_out, o_norm, o_w_in, sgu_ln_w, sgu_ln_b, sgu_ws, sgu_bs, o_w_out, final_norm):
    b, t, d = x.shape
    n = b * t
    depth = e_norm.shape[0] + o_norm.shape[0]
    assert t % 256 == 0 and d == D_MODEL and depth % 2 == 0
    x2d = x.reshape(n, d)
    cos_t, sin_t = _rope_tables(positions)
    e_w_in_b, e_w_out_b = e_w_in.astype(BF16), e_w_out.astype(BF16)
    o_w_in_b, o_w_out_b = o_w_in.astype(BF16), o_w_out.astype(BF16)
    v_first = None
    for layer in range(depth):
        if layer % 2 == 0:
            e = layer // 2
            w_rest = e_w_in_b[e, :, SHIFT_W:]
            r3 = lambda z: z.reshape(b, t, z.shape[-1])
            vres = None if e == 0 else (v_first, rwkv_v0[e - 1], rwkv_v1[e - 1], rwkv_v2[e - 1])
            a_gate, q, kv, b_gate = _norm_matmul(
                x2d, e_norm[e], w_rest, (A_W, B_W, 2 * B_KVW, B_W), tm=512)
            r, k2, v, na, nb, lw, cs, yb = _prep_swa(
                r3(x2d), e_norm[e], e_w_in_b, e, e_mu[e], rwkv_w0[e], rwkv_w2[e], rwkv_a0[e],
                rwkv_a2[e], rwkv_k_k[e], rwkv_k_a[e], vres, r3(q), r3(kv), cos_t, sin_t, r3(b_gate),
                attn_sinks[e], tt=256)
            if e == 0:
                v_first = v
            ya = _wkv(r, k2, v, na, nb, lw, cs, r3(a_gate), rwkv_ln_w[e], rwkv_ln_b[e], rwkv_r_k[e])
            x2d = _out_proj(x2d, ya.reshape(n, A_W), yb.reshape(n, B_W), e_w_out_b, e, tm=512)
        else:
            o = layer // 2
            fin = final_norm if layer == depth - 1 else None
            x2d = _odd_layer(x2d, o_norm[o], o_w_in_b, sgu_ln_w[o], sgu_ln_b[o],
                             sgu_ws[o], sgu_bs[o], o_w_out_b, o, fin, tm=256)
    return x2d.reshape(b, t, d)
```

```python
import functools
import math

import jax
import jax.numpy as jnp
from jax import lax
from jax.experimental import pallas as pl
from jax.experimental.pallas import tpu as pltpu

F32 = jnp.float32
BF16 = jnp.bfloat16

D_MODEL = 2048
HEAD = 64
A_W = 1024
B_W = 1024
B_KVW = 256
B_GROUP = 4
LORA = 64
SHIFT_W = 3 * A_W + 2 * LORA
WINDOW = 128
ROPE_DIM = 16
ROPE_THETA = 500000.0
SGU_CHUNK = 128
SGU_GROUPS = 16
RMS_EPS = 1e-5
LN_EPS = 1e-5
LN_X_EPS = HEAD * 1e-5

LANES = 128
WKV_CHUNK = 64
WKV_GROUP = 4
WKV_STEP_CHUNKS = 4
VMEM_LIMIT = 56 * 1024 * 1024


def _cparams(n_axes):
    return pltpu.CompilerParams(
        dimension_semantics=("arbitrary",) * n_axes, vmem_limit_bytes=VMEM_LIMIT)


def _bdot(a, b):
    return jnp.dot(a.astype(BF16), b.astype(BF16), preferred_element_type=F32)


def _bdot_nt(a, b):
    return lax.dot_general(a.astype(BF16), b.astype(BF16), (((1,), (1,)), ((), ())),
                           preferred_element_type=F32)


def _bdot_tn(a, b):
    return lax.dot_general(a.astype(BF16), b.astype(BF16), (((0,), (0,)), ((), ())),
                           preferred_element_type=F32)


def _split_dot(x, ones_bf16, passes):
    acc = None
    rem = x
    for _ in range(passes):
        piece = rem.astype(BF16)
        term = jnp.dot(piece, ones_bf16, preferred_element_type=F32)
        acc = term if acc is None else acc + term
        rem = rem - piece.astype(F32)
    return acc


def _silu(x):
    return x * (1.0 / (1.0 + jnp.exp(-x)))


def _sigmoid(x):
    return 1.0 / (1.0 + jnp.exp(-x))


def _head_masks():
    lane = lax.broadcasted_iota(jnp.int32, (1, LANES), 1)
    return lane < HEAD, lane >= HEAD


def _norm_matmul_kernel(x_ref, g_ref, w_ref, *out_refs, seg_widths, col_chunk):
    x = x_ref[...]
    ms = jnp.mean(x * x, axis=-1, keepdims=True)
    h = ((x * lax.rsqrt(ms + RMS_EPS)) * g_ref[...]).astype(BF16)
    col = 0
    for o_ref, width in zip(out_refs, seg_widths):
        for c0 in range(0, width, col_chunk):
            cw = min(col_chunk, width - c0)
            o_ref[:, c0:c0 + cw] = jnp.dot(
                h, w_ref[:, col + c0:col + c0 + cw], preferred_element_type=F32)
        col += width


def _norm_matmul(x2d, g, w_bf16, seg_widths, tm):
    n, d = x2d.shape
    ncol = w_bf16.shape[1]
    assert sum(seg_widths) == ncol and n % tm == 0
    kern = functools.partial(_norm_matmul_kernel, seg_widths=tuple(seg_widths), col_chunk=512)
    return pl.pallas_call(
        kern,
        grid=(n // tm,),
        in_specs=[
            pl.BlockSpec((tm, d), lambda i: (i, 0)),
            pl.BlockSpec((1, d), lambda i: (0, 0)),
            pl.BlockSpec((d, ncol), lambda i: (0, 0), pipeline_mode=pl.Buffered(1)),
        ],
        out_specs=[pl.BlockSpec((tm, w), lambda i: (i, 0)) for w in seg_widths],
        out_shape=[jax.ShapeDtypeStruct((n, w), F32) for w in seg_widths],
        compiler_params=_cparams(1),
        name="norm_matmul",
    )(x2d, g.reshape(1, d), w_bf16)


def _prep_swa_kernel(*refs, has_vres):
    (x_ref, g_ref, w_ref, mu_ref, w0_ref, a0_ref, wa2_ref, kk_ref, ka_ref, tri_ref, ones_ref) = refs[:11]
    if has_vres:
        vf_ref, v0_ref, v1_ref, v2_ref = refs[11:15]
    (q_ref, kvc_ref, kvp_ref, cosc_ref, sinc_ref, cosp_ref, sinp_ref, bg_ref, sink_ref,
     r_out, k_out, v_out, na_out, nb_out, lw_out, cs_out, yb_out, carry_ref) = refs[-18:]

    @pl.when(pl.program_id(1) == 0)
    def _():
        carry_ref[...] = jnp.zeros_like(carry_ref)

    x = x_ref[0]
    tt = x.shape[0]

    consts = _swa_consts()
    first8 = consts[2]
    blk = WINDOW
    n_blk = tt // blk
    k_roped, v_rows = [], []
    for j in range(n_blk + 1):
        if j == 0:
            kv, cos_j, sin_j = kvp_ref[0], cosp_ref[0], sinp_ref[0]
        else:
            rows = slice((j - 1) * blk, j * blk)
            kv, cos_j, sin_j = kvc_ref[0, rows], cosc_ref[0, rows], sinc_ref[0, rows]
        k_roped.append([_rope(kv[:, _lane_tile(kt)], cos_j, sin_j, first8)
                        for kt in range(B_KVW // LANES)])
        v_rows.append([kv[:, B_KVW + kt * LANES:B_KVW + (kt + 1) * LANES]
                       for kt in range(B_KVW // LANES)])
    attn = []
    for j in range(n_blk):
        rows = slice(j * blk, (j + 1) * blk)
        k_win = [jnp.concatenate([k_roped[j][kt], k_roped[j + 1][kt]], axis=0)
                 for kt in range(B_KVW // LANES)]
        v_win = [jnp.concatenate([v_rows[j][kt], v_rows[j + 1][kt]], axis=0)
                 for kt in range(B_KVW // LANES)]
        attn.append(_swa_scores(lambda qt, rows=rows: q_ref[0, rows, _lane_tile(qt)], k_win, v_win,
                                cosc_ref[0, rows], sinc_ref[0, rows], consts))

    ms = jnp.mean(x * x, axis=-1, keepdims=True)
    h = ((x * lax.rsqrt(ms + RMS_EPS)) * g_ref[...]).astype(BF16)
    row = lax.broadcasted_iota(jnp.int32, (tt, 1), 0)

    def shifted(c0, width):
        z = jnp.dot(h, w_ref[:, c0:c0 + width], preferred_element_type=F32)
        prev = jnp.where(row == 0, carry_ref[0:1, c0:c0 + width], pltpu.roll(z, 1, axis=0))
        carry_ref[0:1, c0:c0 + width] = z[tt - 1:tt, :]
        return z + (prev - z) * mu_ref[:, c0:c0 + width]

    xwa = shifted(3 * A_W, 2 * LORA)
    lane = lax.broadcasted_iota(jnp.int32, (1, 2 * LORA), 1)
    lora_in = jnp.where(lane < LORA, jnp.tanh(xwa), xwa)
    lora = _bdot(lora_in, wa2_ref[...])
    wpre = w0_ref[...] + lora[:, 0:A_W]
    neg = -wpre
    softplus = jnp.maximum(neg, 0.0) + jnp.log(1.0 + jnp.exp(-jnp.abs(neg)))
    w_log = -softplus - 0.5
    lw = -jnp.exp(w_log)
    a = _sigmoid(a0_ref[...] + lora[:, A_W:2 * A_W])
    lw_out[0] = lw
    v = shifted(2 * A_W, A_W)
    if has_vres:
        vf = vf_ref[0]
        gate = _sigmoid(v0_ref[...] + _bdot(_bdot(v, v1_ref[...]), v2_ref[...]))
        v = v + (vf - v) * gate
    v_out[0] = v
    r_out[0] = shifted(0, A_W)
    hi = lw.astype(BF16)
    lo = (lw - hi.astype(F32)).astype(BF16)
    tri = tri_ref[...]
    cs_out[0] = (jnp.dot(tri, hi, preferred_element_type=F32)
                 + jnp.dot(tri, lo, preferred_element_type=F32))
    k = shifted(A_W, A_W)
    k_out[0] = k * (1.0 + (a - 1.0) * ka_ref[...])
    kk = k * kk_ref[...]
    ones_bd = ones_ref[...]
    gw = ones_bd.shape[0]
    for p in range(A_W // gw):
        sl = slice(p * gw, (p + 1) * gw)
        kkp = kk[:, sl]
        ss = _bdot(kkp * kkp, ones_bd)
        kkn = kkp * lax.rsqrt(ss + 1e-12)
        na_out[0, :, sl] = -kkn
        nb_out[0, :, sl] = kkn * a[:, sl]

    for j in range(n_blk):
        rows = slice(j * blk, (j + 1) * blk)
        has_prev = (pl.program_id(1) > 0) if j == 0 else True

        def store_tile(qt, val, rows=rows):
            yb_out[0, rows, _lane_tile(qt)] = val

        _swa_outputs(attn[j][0], attn[j][1], has_prev, sink_ref,
                     lambda qt, rows=rows: bg_ref[0, rows, _lane_tile(qt)], store_tile, consts)


def _prep_swa(x3d, g, w_in_stack, layer, mu, w0, w2, a0, a2, k_k, k_a, vres, q, kv, cos_t, sin_t,
              b_gate, sinks, tt):
    b, t, d = x3d.shape
    has_vres = vres is not None
    wa2 = jnp.zeros((2 * LORA, 2 * A_W), F32)
    wa2 = wa2.at[:LORA, :A_W].set(w2).at[LORA:, A_W:].set(a2).astype(BF16)
    ti = jnp.arange(tt)
    tri = ((ti[:, None] >= ti[None, :]) &
           (ti[:, None] // WKV_CHUNK == ti[None, :] // WKV_CHUNK)).astype(BF16)
    gw = WKV_GROUP * HEAD
    li = jnp.arange(gw)
    ones_bd = (li[:, None] // HEAD == li[None, :] // HEAD).astype(BF16)
    row = lambda p: p.reshape(1, -1)
    full = lambda shape: pl.BlockSpec(shape, lambda bi, i: (0,) * len(shape))
    tile = lambda w: pl.BlockSpec((1, tt, w), lambda bi, i: (bi, i, 0))
    in_specs = [
        tile(d), full((1, d)),
        pl.BlockSpec((None, d, SHIFT_W), lambda bi, i: (layer, 0, 0), pipeline_mode=pl.Buffered(1)),
        full((1, SHIFT_W)), full((1, A_W)), full((1, A_W)), full((2 * LORA, 2 * A_W)),
        full((1, A_W)), full((1, A_W)), full((tt, tt)), full((gw, gw)),
    ]
    args = [x3d, row(g), w_in_stack, row(mu), row(w0), row(a0), wa2, row(k_k), row(k_a), tri, ones_bd]
    if has_vres:
        v_first, v0, v1, v2 = vres
        in_specs += [tile(A_W), full((1, A_W)), full(v1.shape), full(v2.shape)]
        args += [v_first, row(v0), v1.astype(BF16), v2.astype(BF16)]
    blk = WINDOW
    prev_blk = lambda w: pl.BlockSpec(
        (1, blk, w), lambda bi, i: (bi, jnp.maximum(i * (tt // blk) - 1, 0), 0))
    in_specs += [tile(B_W), tile(2 * B_KVW), prev_blk(2 * B_KVW), tile(LANES), tile(LANES),
                 prev_blk(LANES), prev_blk(LANES), tile(B_W), full((1, B_W))]
    args += [q, kv, kv, cos_t, sin_t, cos_t, sin_t, b_gate,
             jnp.repeat(sinks.astype(F32), HEAD).reshape(1, B_W)]
    return pl.pallas_call(
        functools.partial(_prep_swa_kernel, has_vres=has_vres),
        grid=(b, t // tt),
        in_specs=in_specs,
        out_specs=[tile(A_W)] * 7 + [tile(B_W)],
        out_shape=[jax.ShapeDtypeStruct((b, t, A_W), F32)] * 7 + [jax.ShapeDtypeStruct((b, t, B_W), F32)],
        scratch_shapes=[pltpu.VMEM((8, SHIFT_W), F32)],
        compiler_params=_cparams(2),
        name="prep_swa",
    )(*args)


def _wkv_kernel(r_ref, k_ref, v_ref, na_ref, nb_ref, lw_ref, cs_ref, g_ref,
                lnw_ref, lnb_ref, rk_ref, ones_ref, o_ref, s_ref):
    c = WKV_CHUNK

    @pl.when(pl.program_id(0) == 0)
    def _():
        s_ref[...] = jnp.zeros_like(s_ref)

    gw = WKV_GROUP * HEAD
    lane = lax.broadcasted_iota(jnp.int32, (1, gw), 1)
    head_masks = [(lane >= h * HEAD) & (lane < (h + 1) * HEAD) for h in range(WKV_GROUP)]
    ti = lax.broadcasted_iota(jnp.int32, (c, gw), 0)
    tj = lax.broadcasted_iota(jnp.int32, (c, gw), 1) & (c - 1)
    strict = ti > tj
    incl = ti >= tj
    eye_t = jnp.where(ti == tj, 1.0, 0.0)
    li = lax.broadcasted_iota(jnp.int32, (gw, gw), 0)
    lj = lax.broadcasted_iota(jnp.int32, (gw, gw), 1)
    bdmask = (li // HEAD) == (lj // HEAD)
    ones_bd = ones_ref[...]

    def bd(x):
        xb = x.astype(BF16)
        zero = jnp.zeros_like(xb)
        return jnp.concatenate([jnp.where(m, xb, zero) for m in head_masks], axis=0)

    n_ck = r_ref.shape[1] // c
    n_state = s_ref.shape[0]
    chains = [(bi, slice(ck * c, (ck + 1) * c), slice(p * gw, (p + 1) * gw))
              for ck in range(n_ck) for bi in range(r_ref.shape[0]) for p in range(A_W // gw)]
    pairs = range(len(chains))
    rd = lambda ref, p: ref[chains[p]]
    rt, at_abs, bh, kh, wc, em, lhs0, rhs0 = [], [], [], [], [], [], [], []
    for p in pairs:
        cs = rd(cs_ref, p)
        mid = cs[c // 2 - 1:c // 2, :]
        last = cs[c - 1:c, :]
        e_neg = jnp.exp(mid - cs)
        rt_p = rd(r_ref, p) * jnp.exp(cs - mid)
        at_p = rd(na_ref, p) * jnp.exp(cs - rd(lw_ref, p) - mid)
        kt_p = rd(k_ref, p) * e_neg
        bt_p = rd(nb_ref, p) * e_neg
        em_p = jnp.exp(mid)
        e_end = jnp.exp(last - mid)
        rt.append(rt_p)
        at_abs.append(at_p * em_p)
        em.append(em_p)
        wc.append(jnp.exp(last))
        bh.append(bt_p * e_end)
        kh.append(kt_p * e_end)
        lhs0.append(jnp.concatenate([at_p, rt_p], axis=0))
        rhs0.append(jnp.concatenate([bd(bt_p), bd(kt_p)], axis=0))
    sc = [_bdot_nt(lhs0[p], rhs0[p]) for p in pairs]
    a_ab = [jnp.where(strict, sc[p][:c, :gw], 0.0) for p in pairs]
    a_rb = [jnp.where(incl, sc[p][c:, :gw], 0.0) for p in pairs]
    akrk = [jnp.concatenate([jnp.where(strict, sc[p][:c, gw:], 0.0),
                             jnp.where(incl, sc[p][c:, gw:], 0.0)], axis=0) for p in pairs]
    xv = [_bdot(akrk[p], bd(rd(v_ref, p))) for p in pairs]
    pinv = [eye_t + a_ab[p] for p in pairs]
    apow = [_bdot(a_ab[p], bd(a_ab[p])) for p in pairs]
    for _ in range(4):
        both = [_bdot(jnp.concatenate([apow[p], pinv[p]], axis=0), bd(apow[p])) for p in pairs]
        pinv = [pinv[p] + both[p][c:] for p in pairs]
        apow = [both[p][:c] for p in pairs]
    pinv = [pinv[p] + _bdot(pinv[p], bd(apow[p])) for p in pairs]
    ta = [_bdot(pinv[p], jnp.concatenate([bd(xv[p][:c]), bd(at_abs[p])], axis=1)) for p in pairs]
    uv = [ta[p][:, :gw] for p in pairs]
    ap = [ta[p][:, gw:] for p in pairs]
    rb = [_bdot(a_rb[p], jnp.concatenate([bd(ap[p]), bd(uv[p])], axis=1)) for p in pairs]
    rpap = [jnp.concatenate([rt[p] * em[p] + rb[p][:, :gw], ap[p]], axis=0) for p in pairs]
    yv = [rb[p][:, gw:] + xv[p][c:] for p in pairs]
    bk = [jnp.concatenate([bh[p], kh[p]], axis=0) for p in pairs]
    state = [s_ref[si] for si in range(n_state)]
    y = []
    for ck in range(n_ck):
        ps = [ck * n_state + si for si in range(n_state)]
        ru = [_bdot_nt(rpap[p], state[si]) for si, p in enumerate(ps)]
        y += [ru[si][:c] + yv[p] for si, p in enumerate(ps)]
        grams = [_bdot_tn(jnp.concatenate([ru[si][c:] + uv[p], rd(v_ref, p)], axis=0), bk[p])
                 for si, p in enumerate(ps)]
        state = [state[si] * wc[p] + jnp.where(bdmask, grams[si], 0.0) for si, p in enumerate(ps)]
    for si in range(n_state):
        s_ref[si] = state[si]
    stat_rows = []
    for p in pairs:
        stat_rows += [y[p], rd(r_ref, p) * rd(k_ref, p) * rk_ref[:, chains[p][2]]]
    stats_all = _bdot(jnp.concatenate(stat_rows, axis=0), ones_bd)
    stats = [stats_all[2 * c * p:2 * c * (p + 1)] for p in pairs]
    d = [y[p] - stats[p][:c] * (1.0 / HEAD) for p in pairs]
    var_all = _bdot(jnp.concatenate([d[p] * d[p] for p in pairs], axis=0), ones_bd) * (1.0 / HEAD)
    var = [var_all[c * p:c * (p + 1)] for p in pairs]
    for p in pairs:
        sl = chains[p][2]
        yn = d[p] * lax.rsqrt(var[p] + LN_X_EPS) * lnw_ref[:, sl] + lnb_ref[:, sl]
        out = yn + stats[p][c:] * rd(v_ref, p)
        o_ref[chains[p]] = out * _silu(rd(g_ref, p))


def _wkv(r, k, v, na, nb, lw, cs, gate, ln_w, ln_b, r_k):
    b, t, _ = r.shape
    c = WKV_CHUNK
    gw = WKV_GROUP * HEAD
    li = jnp.arange(gw)
    ones_bd = (li[:, None] // HEAD == li[None, :] // HEAD).astype(BF16)
    rows = WKV_STEP_CHUNKS * c
    assert t % rows == 0
    tile = pl.BlockSpec((b, rows, A_W), lambda i: (0, i, 0))
    vec = pl.BlockSpec((1, A_W), lambda i: (0, 0))
    return pl.pallas_call(
        _wkv_kernel,
        grid=(t // rows,),
        in_specs=[tile] * 8 + [vec] * 3 + [pl.BlockSpec((gw, gw), lambda i: (0, 0))],
        out_specs=tile,
        out_shape=jax.ShapeDtypeStruct((b, t, A_W), F32),
        scratch_shapes=[pltpu.VMEM((b * (A_W // gw), gw, gw), F32)],
        compiler_params=_cparams(1),
        name="wkv",
    )(r, k, v, na, nb, lw, cs, gate, ln_w.reshape(1, A_W), ln_b.reshape(1, A_W),
      r_k.reshape(1, A_W), ones_bd)


def _rope_angles_kernel(pos_ref, freq_ref, cos_ref, sin_ref, nsin_ref):
    ang = pos_ref[...].astype(F32) * freq_ref[...]
    sin = jnp.sin(ang)
    cos_ref[...] = jnp.cos(ang)
    sin_ref[...] = sin
    nsin_ref[...] = -sin


def _rope_tables(positions):
    b, t = positions.shape
    half = ROPE_DIM // 2
    n_rows = b * t * half // LANES
    inv_freq = jnp.power(jnp.float32(ROPE_THETA), -jnp.arange(half, dtype=F32) / half)
    pos_rep = jnp.repeat(positions.reshape(-1), half).reshape(n_rows, LANES)
    freq = jnp.tile(inv_freq, LANES // half).reshape(1, LANES)
    whole = pl.BlockSpec((n_rows, LANES), lambda i: (0, 0))
    cos_c, sin_c, nsin_c = pl.pallas_call(
        _rope_angles_kernel,
        grid=(1,),
        in_specs=[whole, pl.BlockSpec((1, LANES), lambda i: (0, 0))],
        out_specs=[whole] * 3,
        out_shape=[jax.ShapeDtypeStruct((n_rows, LANES), F32)] * 3,
        compiler_params=_cparams(1),
        name="rope_angles",
    )(pos_rep, freq)
    cos8, sin8, nsin8 = (z.reshape(b, t, half) for z in (cos_c, sin_c, nsin_c))
    rest = HEAD - ROPE_DIM
    cos_head = jnp.concatenate([cos8, cos8, jnp.ones((b, t, rest), F32)], axis=-1)
    sin_head = jnp.concatenate([nsin8, sin8, jnp.zeros((b, t, rest), F32)], axis=-1)
    n_heads = LANES // HEAD
    return jnp.tile(cos_head, (1, 1, n_heads)), jnp.tile(sin_head, (1, 1, n_heads))


def _rope(x, cos_t, sin_t, first8):
    fwd = pltpu.roll(x, LANES - ROPE_DIM // 2, axis=1)
    bwd = pltpu.roll(x, ROPE_DIM // 2, axis=1)
    return x * cos_t + jnp.where(first8, fwd, bwd) * sin_t


def _lane_tile(j):
    return slice(j * LANES, (j + 1) * LANES)


def _swa_consts():
    blk = WINDOW
    m0, m1 = _head_masks()
    lane = lax.broadcasted_iota(jnp.int32, (1, LANES), 1)
    first8 = (lane & (HEAD - 1)) < ROPE_DIM // 2
    ki = lax.broadcasted_iota(jnp.int32, (2 * blk, blk), 0)
    qi = lax.broadcasted_iota(jnp.int32, (2 * blk, blk), 1) + blk
    band = (ki <= qi) & (qi - ki < WINDOW)
    in_cur = ki >= blk
    return m0, m1, first8, band, in_cur


def _swa_scores(q_tile_at, k_win, v_win, cos_c, sin_c, consts):
    m0, m1, first8, _, _ = consts
    scale = 1.0 / math.sqrt(HEAD)
    values_t = [v_win[kt].T for kt in range(B_KVW // LANES)]
    scores_t = []
    for g in range(B_KVW // HEAD):
        k_tile = k_win[g // 2]
        swapped = pltpu.roll(k_tile, HEAD, axis=1)
        k_dup = jnp.where(m0, k_tile, swapped) if g % 2 == 0 else jnp.where(m0, swapped, k_tile)
        rows = []
        for pr in range(B_GROUP // 2):
            qp = _rope(q_tile_at(2 * g + pr), cos_c, sin_c, first8) * scale
            rows += [jnp.where(m0, qp, 0.0), jnp.where(m1, qp, 0.0)]
        scores_t.append(_bdot_nt(k_dup, jnp.concatenate(rows, axis=0)))
    return scores_t, values_t


def _swa_outputs(scores_t, values_t, has_prev, sink_ref, gate_tile_at, store_tile, consts):
    _, _, _, band, in_cur = consts
    blk = WINDOW
    valid = band & (has_prev | in_cur)
    ones_rows = jnp.ones((16, 2 * blk), BF16)
    for g in range(B_KVW // HEAD):
        v_t = values_t[g // 2][(g % 2) * HEAD:(g % 2 + 1) * HEAD]
        lhs = jnp.concatenate([v_t.astype(BF16), ones_rows], axis=0)
        for pr in range(B_GROUP // 2):
            qt = 2 * g + pr
            halves = []
            for half in range(2):
                c0 = (2 * pr + half) * blk
                s = jnp.where(valid, scores_t[g][:, c0:c0 + blk], -jnp.inf)
                sink = sink_ref[:, qt * LANES + half * HEAD:qt * LANES + half * HEAD + 1]
                m = jnp.maximum(jnp.max(s, axis=0, keepdims=True), sink)
                p = jnp.exp(s - m).astype(BF16)
                od = jnp.dot(lhs, p, preferred_element_type=F32)
                den = od[HEAD:HEAD + 1] + jnp.exp(sink - m)
                halves.append(od[:HEAD] * (1.0 / den))
            o_tile = jnp.concatenate(halves, axis=0).T
            store_tile(qt, o_tile * _silu(gate_tile_at(qt)))


def _out_proj_kernel(x_ref, ya_ref, yb_ref, w_ref, o_ref):
    y = jnp.concatenate([ya_ref[...].astype(BF16), yb_ref[...].astype(BF16)], axis=1)
    o_ref[...] = x_ref[...] + jnp.dot(y, w_ref[...], preferred_element_type=F32)


def _out_proj(x2d, ya, yb, w_stack, layer, tm):
    n, d = x2d.shape
    return pl.pallas_call(
        _out_proj_kernel,
        grid=(n // tm,),
        in_specs=[
            pl.BlockSpec((tm, d), lambda i: (i, 0)),
            pl.BlockSpec((tm, A_W), lambda i: (i, 0)),
            pl.BlockSpec((tm, B_W), lambda i: (i, 0)),
            pl.BlockSpec((None, A_W + B_W, d), lambda i: (layer, 0, 0), pipeline_mode=pl.Buffered(1)),
        ],
        out_specs=pl.BlockSpec((tm, d), lambda i: (i, 0)),
        out_shape=jax.ShapeDtypeStruct((n, d), F32),
        compiler_params=_cparams(1),
        name="out_proj",
    )(x2d, ya, yb, w_stack)


def _odd_layer_kernel(x_ref, g_ref, win_ref, lnw_ref, lnb_ref, ws_ref, bst_ref, wout_ref,
                      fin_ref, o_ref, y_ref, *, final_norm):
    tm, d = x_ref.shape
    ch = SGU_CHUNK
    x = x_ref[...]
    ms = jnp.mean(x * x, axis=-1, keepdims=True)
    h = ((x * lax.rsqrt(ms + RMS_EPS)) * g_ref[...]).astype(BF16)
    v = jnp.dot(h, win_ref[:, d:2 * d], preferred_element_type=F32)
    mean = jnp.mean(v, axis=-1, keepdims=True)
    dv = v - mean
    var = jnp.mean(dv * dv, axis=-1, keepdims=True)
    vn = ((dv * lax.rsqrt(var + LN_EPS)) * lnw_ref[...] + lnb_ref[...]).astype(BF16)
    ti = lax.broadcasted_iota(jnp.int32, (ch, ch), 0)
    si = lax.broadcasted_iota(jnp.int32, (ch, ch), 1)
    causal = ti >= si
    cw = 2 * LANES
    zero_tile = jnp.zeros((ch, LANES), BF16)
    for j in range(d // cw):
        c0 = j * cw
        u = jnp.dot(h, win_ref[:, c0:c0 + cw], preferred_element_type=F32)
        gate = jnp.dot(h, win_ref[:, 2 * d + c0:2 * d + c0 + cw], preferred_element_type=F32)
        wm = jnp.concatenate(
            [jnp.where(causal, ws_ref[2 * j + gi], 0.0).astype(BF16) for gi in range(2)], axis=1)
        bias = jnp.concatenate(
            [jnp.broadcast_to(bst_ref[:, 2 * j + gi:2 * j + gi + 1], (ch, LANES)) for gi in range(2)],
            axis=1)
        mixed = []
        for ci in range(tm // ch):
            vc = vn[ci * ch:(ci + 1) * ch, c0:c0 + cw]
            v_bd = jnp.concatenate(
                [jnp.concatenate([vc[:, :LANES], zero_tile], axis=1),
                 jnp.concatenate([zero_tile, vc[:, LANES:]], axis=1)], axis=0)
            mixed.append(jnp.dot(wm, v_bd, preferred_element_type=F32) + bias)
        y_ref[:, c0:c0 + cw] = (u * jnp.concatenate(mixed, axis=0) * _silu(gate)).astype(BF16)
    out = x + jnp.dot(y_ref[...], wout_ref[...], preferred_element_type=F32)
    if final_norm:
        ms = jnp.mean(out * out, axis=-1, keepdims=True)
        out = (out * lax.rsqrt(ms + RMS_EPS)) * fin_ref[...]
    o_ref[...] = out


def _odd_layer(x2d, g, w_in_stack, ln_w, ln_b, ws, bs, w_out_stack, layer, final_g, tm):
    n, d = x2d.shape
    final_norm = final_g is not None
    fin = final_g if final_norm else jnp.ones((d,), F32)
    row_tile = pl.BlockSpec((tm, d), lambda i: (i, 0))
    vec = pl.BlockSpec((1, d), lambda i: (0, 0))
    return pl.pallas_call(
        functools.partial(_odd_layer_kernel, final_norm=final_norm),
        grid=(n // tm,),
        in_specs=[
            row_tile, vec,
            pl.BlockSpec((None, d, 3 * d), lambda i: (layer, 0, 0), pipeline_mode=pl.Buffered(1)),
            vec, vec,
            pl.BlockSpec((SGU_GROUPS, SGU_CHUNK, SGU_CHUNK), lambda i: (0, 0, 0)),
            pl.BlockSpec((SGU_CHUNK, SGU_GROUPS), lambda i: (0, 0)),
            pl.BlockSpec((None, d, d), lambda i: (layer, 0, 0), pipeline_mode=pl.Buffered(1)),
            vec,
        ],
        out_specs=row_tile,
        out_shape=jax.ShapeDtypeStruct((n, d), F32),
        scratch_shapes=[pltpu.VMEM((tm, d), BF16)],
        compiler_params=_cparams(1),
        name="odd_layer",
    )(x2d, g.reshape(1, d), w_in_stack, ln_w.reshape(1, d), ln_b.reshape(1, d), ws, bs.T,
      w_out_stack, fin.reshape(1, d))


def kernel(x, positions, e_norm, e_w_in, e_mu, rwkv_w0, rwkv_w2, rwkv_a0, rwkv_a2, rwkv_k_k, rwkv_k_a, rwkv_r_k, rwkv_ln_w, rwkv_ln_b, rwkv_v0, rwkv_v1, rwkv_v2, attn_sinks, e_w_out, o_norm, o_w_in, sgu_ln_w, sgu_ln_b, sgu_ws, sgu_bs, o_w_out, final_norm):
    b, t, d = x.shape
    n = b * t
    depth = e_norm.shape[0] + o_norm.shape[0]
    assert t % 256 == 0 and d == D_MODEL and depth % 2 == 0
    x2d = x.reshape(n, d)
    cos_t, sin_t = _rope_tables(positions)
    e_w_in_b, e_w_out_b = e_w_in.astype(BF16), e_w_out.astype(BF16)
    o_w_in_b, o_w_out_b = o_w_in.astype(BF16), o_w_out.astype(BF16)
    v_first = None
    for layer in range(depth):
        if layer % 2 == 0:
            e = layer // 2
            w_rest = e_w_in_b[e, :, SHIFT_W:]
            r3 = lambda z: z.reshape(b, t, z.shape[-1])
            vres = None if e == 0 else (v_first, rwkv_v0[e - 1], rwkv_v1[e - 1], rwkv_v2[e - 1])
            a_gate, q, kv, b_gate = _norm_matmul(
                x2d, e_norm[e], w_rest, (A_W, B_W, 2 * B_KVW, B_W), tm=512)
            r, k2, v, na, nb, lw, cs, yb = _prep_swa(
                r3(x2d), e_norm[e], e_w_in_b, e, e_mu[e], rwkv_w0[e], rwkv_w2[e], rwkv_a0[e],
                rwkv_a2[e], rwkv_k_k[e], rwkv_k_a[e], vres, r3(q), r3(kv), cos_t, sin_t, r3(b_gate),
                attn_sinks[e], tt=256)
            if e == 0:
                v_first = v
            ya = _wkv(r, k2, v, na, nb, lw, cs, r3(a_gate), rwkv_ln_w[e], rwkv_ln_b[e], rwkv_r_k[e])
            x2d = _out_proj(x2d, ya.reshape(n, A_W), yb.reshape(n, B_W), e_w_out_b, e, tm=512)
        else:
            o = layer // 2
            fin = final_norm if layer == depth - 1 else None
            x2d = _odd_layer(x2d, o_norm[o], o_w_in_b, sgu_ln_w[o], sgu_ln_b[o],
                             sgu_ws[o], sgu_bs[o], o_w_out_b, o, fin, tm=256)
    return x2d.reshape(b, t, d)
```

```python
import functools
import math

import jax
import jax.numpy as jnp
from jax import lax
from jax.experimental import pallas as pl
from jax.experimental.pallas import tpu as pltpu

F32 = jnp.float32
BF16 = jnp.bfloat16

D_MODEL = 2048
HEAD = 64
A_W = 1024
B_W = 1024
B_KVW = 256
B_GROUP = 4
LORA = 64
SHIFT_W = 3 * A_W + 2 * LORA
WINDOW = 128
ROPE_DIM = 16
ROPE_THETA = 500000.0
SGU_CHUNK = 128
SGU_GROUPS = 16
RMS_EPS = 1e-5
LN_EPS = 1e-5
LN_X_EPS = HEAD * 1e-5

LANES = 128
WKV_CHUNK = 64
WKV_GROUP = 4
WKV_STEP_CHUNKS = 4
VMEM_LIMIT = 56 * 1024 * 1024


def _cparams(n_axes):
    return pltpu.CompilerParams(
        dimension_semantics=("arbitrary",) * n_axes, vmem_limit_bytes=VMEM_LIMIT)


def _bdot(a, b):
    return jnp.dot(a.astype(BF16), b.astype(BF16), preferred_element_type=F32)


def _bdot_nt(a, b):
    return lax.dot_general(a.astype(BF16), b.astype(BF16), (((1,), (1,)), ((), ())),
                           preferred_element_type=F32)


def _bdot_tn(a, b):
    return lax.dot_general(a.astype(BF16), b.astype(BF16), (((0,), (0,)), ((), ())),
                           preferred_element_type=F32)


def _split_dot(x, ones_bf16, passes):
    acc = None
    rem = x
    for _ in range(passes):
        piece = rem.astype(BF16)
        term = jnp.dot(piece, ones_bf16, preferred_element_type=F32)
        acc = term if acc is None else acc + term
        rem = rem - piece.astype(F32)
    return acc


def _silu(x):
    return x * (1.0 / (1.0 + jnp.exp(-x)))


def _sigmoid(x):
    return 1.0 / (1.0 + jnp.exp(-x))


def _head_masks():
    lane = lax.broadcasted_iota(jnp.int32, (1, LANES), 1)
    return lane < HEAD, lane >= HEAD


def _norm_matmul_kernel(x_ref, g_ref, w_ref, *out_refs, seg_widths, col_chunk):
    x = x_ref[...]
    ms = jnp.mean(x * x, axis=-1, keepdims=True)
    h = ((x * lax.rsqrt(ms + RMS_EPS)) * g_ref[...]).astype(BF16)
    col = 0
    for o_ref, width in zip(out_refs, seg_widths):
        for c0 in range(0, width, col_chunk):
            cw = min(col_chunk, width - c0)
            o_ref[:, c0:c0 + cw] = jnp.dot(
                h, w_ref[:, col + c0:col + c0 + cw], preferred_element_type=F32)
        col += width


def _norm_matmul(x2d, g, w_bf16, seg_widths, tm):
    n, d = x2d.shape
    ncol = w_bf16.shape[1]
    assert sum(seg_widths) == ncol and n % tm == 0
    kern = functools.partial(_norm_matmul_kernel, seg_widths=tuple(seg_widths), col_chunk=512)
    return pl.pallas_call(
        kern,
        grid=(n // tm,),
        in_specs=[
            pl.BlockSpec((tm, d), lambda i: (i, 0)),
            pl.BlockSpec((1, d), lambda i: (0, 0)),
            pl.BlockSpec((d, ncol), lambda i: (0, 0), pipeline_mode=pl.Buffered(1)),
        ],
        out_specs=[pl.BlockSpec((tm, w), lambda i: (i, 0)) for w in seg_widths],
        out_shape=[jax.ShapeDtypeStruct((n, w), F32) for w in seg_widths],
        compiler_params=_cparams(1),
        name="norm_matmul",
    )(x2d, g.reshape(1, d), w_bf16)


def _prep_swa_kernel(*refs, has_vres):
    (x_ref, g_ref, w_ref, mu_ref, w0_ref, a0_ref, wa2_ref, kk_ref, ka_ref, tri_ref, ones_ref) = refs[:11]
    if has_vres:
        vf_ref, v0_ref, v1_ref, v2_ref = refs[11:15]
    (q_ref, kvc_ref, kvp_ref, cosc_ref, sinc_ref, cosp_ref, sinp_ref, bg_ref, sink_ref,
     r_out, k_out, v_out, na_out, nb_out, lw_out, cs_out, yb_out, carry_ref) = refs[-18:]

    @pl.when(pl.program_id(1) == 0)
    def _():
        carry_ref[...] = jnp.zeros_like(carry_ref)

    x = x_ref[0]
    tt = x.shape[0]

    consts = _swa_consts()
    first8 = consts[2]
    blk = WINDOW
    n_blk = tt // blk
    k_roped, v_rows = [], []
    for j in range(n_blk + 1):
        if j == 0:
            kv, cos_j, sin_j = kvp_ref[0], cosp_ref[0], sinp_ref[0]
        else:
            rows = slice((j - 1) * blk, j * blk)
            kv, cos_j, sin_j = kvc_ref[0, rows], cosc_ref[0, rows], sinc_ref[0, rows]
        k_roped.append([_rope(kv[:, _lane_tile(kt)], cos_j, sin_j, first8)
                        for kt in range(B_KVW // LANES)])
        v_rows.append([kv[:, B_KVW + kt * LANES:B_KVW + (kt + 1) * LANES]
                       for kt in range(B_KVW // LANES)])
    attn = []
    for j in range(n_blk):
        rows = slice(j * blk, (j + 1) * blk)
        k_win = [jnp.concatenate([k_roped[j][kt], k_roped[j + 1][kt]], axis=0)
                 for kt in range(B_KVW // LANES)]
        v_win = [jnp.concatenate([v_rows[j][kt], v_rows[j + 1][kt]], axis=0)
                 for kt in range(B_KVW // LANES)]
        attn.append(_swa_scores(lambda qt, rows=rows: q_ref[0, rows, _lane_tile(qt)], k_win, v_win,
                                cosc_ref[0, rows], sinc_ref[0, rows], consts))

    ms = jnp.mean(x * x, axis=-1, keepdims=True)
    h = ((x * lax.rsqrt(ms + RMS_EPS)) * g_ref[...]).astype(BF16)
    row = lax.broadcasted_iota(jnp.int32, (tt, 1), 0)

    def shifted(c0, width):
        z = jnp.dot(h, w_ref[:, c0:c0 + width], preferred_element_type=F32)
        prev = jnp.where(row == 0, carry_ref[0:1, c0:c0 + width], pltpu.roll(z, 1, axis=0))
        carry_ref[0:1, c0:c0 + width] = z[tt - 1:tt, :]
        return z + (prev - z) * mu_ref[:, c0:c0 + width]

    xwa = shifted(3 * A_W, 2 * LORA)
    lane = lax.broadcasted_iota(jnp.int32, (1, 2 * LORA), 1)
    lora_in = jnp.where(lane < LORA, jnp.tanh(xwa), xwa)
    lora = _bdot(lora_in, wa2_ref[...])
    wpre = w0_ref[...] + lora[:, 0:A_W]
    neg = -wpre
    softplus = jnp.maximum(neg, 0.0) + jnp.log(1.0 + jnp.exp(-jnp.abs(neg)))
    w_log = -softplus - 0.5
    lw = -jnp.exp(w_log)
    a = _sigmoid(a0_ref[...] + lora[:, A_W:2 * A_W])
    lw_out[0] = lw
    v = shifted(2 * A_W, A_W)
    if has_vres:
        vf = vf_ref[0]
        gate = _sigmoid(v0_ref[...] + _bdot(_bdot(v, v1_ref[...]), v2_ref[...]))
        v = v + (vf - v) * gate
    v_out[0] = v
    r_out[0] = shifted(0, A_W)
    hi = lw.astype(BF16)
    lo = (lw - hi.astype(F32)).astype(BF16)
    tri = tri_ref[...]
    cs_out[0] = (jnp.dot(tri, hi, preferred_element_type=F32)
                 + jnp.dot(tri, lo, preferred_element_type=F32))
    k = shifted(A_W, A_W)
    k_out[0] = k * (1.0 + (a - 1.0) * ka_ref[...])
    kk = k * kk_ref[...]
    ones_bd = ones_ref[...]
    gw = ones_bd.shape[0]
    for p in range(A_W // gw):
        sl = slice(p * gw, (p + 1) * gw)
        kkp = kk[:, sl]
        ss = _bdot(kkp * kkp, ones_bd)
        kkn = kkp * lax.rsqrt(ss + 1e-12)
        na_out[0, :, sl] = -kkn
        nb_out[0, :, sl] = kkn * a[:, sl]

    for j in range(n_blk):
        rows = slice(j * blk, (j + 1) * blk)
        has_prev = (pl.program_id(1) > 0) if j == 0 else True

        def store_tile(qt, val, rows=rows):
            yb_out[0, rows, _lane_tile(qt)] = val

        _swa_outputs(attn[j][0], attn[j][1], has_prev, sink_ref,
                     lambda qt, rows=rows: bg_ref[0, rows, _lane_tile(qt)], store_tile, consts)


def _prep_swa(x3d, g, w_in_stack, layer, mu, w0, w2, a0, a2, k_k, k_a, vres, q, kv, cos_t, sin_t,
              b_gate, sinks, tt):
    b, t, d = x3d.shape
    has_vres = vres is not None
    wa2 = jnp.zeros((2 * LORA, 2 * A_W), F32)
    wa2 = wa2.at[:LORA, :A_W].set(w2).at[LORA:, A_W:].set(a2).astype(BF16)
    ti = jnp.arange(tt)
    tri = ((ti[:, None] >= ti[None, :]) &
           (ti[:, None] // WKV_CHUNK == ti[None, :] // WKV_CHUNK)).astype(BF16)
    gw = WKV_GROUP * HEAD
    li = jnp.arange(gw)
    ones_bd = (li[:, None] // HEAD == li[None, :] // HEAD).astype(BF16)
    row = lambda p: p.reshape(1, -1)
    full = lambda shape: pl.BlockSpec(shape, lambda bi, i: (0,) * len(shape))
    tile = lambda w: pl.BlockSpec((1, tt, w), lambda bi, i: (bi, i, 0))
    in_specs = [
        tile(d), full((1, d)),
        pl.BlockSpec((None, d, SHIFT_W), lambda bi, i: (layer, 0, 0), pipeline_mode=pl.Buffered(1)),
        full((1, SHIFT_W)), full((1, A_W)), full((1, A_W)), full((2 * LORA, 2 * A_W)),
        full((1, A_W)), full((1, A_W)), full((tt, tt)), full((gw, gw)),
    ]
    args = [x3d, row(g), w_in_stack, row(mu), row(w0), row(a0), wa2, row(k_k), row(k_a), tri, ones_bd]
    if has_vres:
        v_first, v0, v1, v2 = vres
        in_specs += [tile(A_W), full((1, A_W)), full(v1.shape), full(v2.shape)]
        args += [v_first, row(v0), v1.astype(BF16), v2.astype(BF16)]
    blk = WINDOW
    prev_blk = lambda w: pl.BlockSpec(
        (1, blk, w), lambda bi, i: (bi, jnp.maximum(i * (tt // blk) - 1, 0), 0))
    in_specs += [tile(B_W), tile(2 * B_KVW), prev_blk(2 * B_KVW), tile(LANES), tile(LANES),
                 prev_blk(LANES), prev_blk(LANES), tile(B_W), full((1, B_W))]
    args += [q, kv, kv, cos_t, sin_t, cos_t, sin_t, b_gate,
             jnp.repeat(sinks.astype(F32), HEAD).reshape(1, B_W)]
    return pl.pallas_call(
        functools.partial(_prep_swa_kernel, has_vres=has_vres),
        grid=(b, t // tt),
        in_specs=in_specs,
        out_specs=[tile(A_W)] * 7 + [tile(B_W)],
        out_shape=[jax.ShapeDtypeStruct((b, t, A_W), F32)] * 7 + [jax.ShapeDtypeStruct((b, t, B_W), F32)],
        scratch_shapes=[pltpu.VMEM((8, SHIFT_W), F32)],
        compiler_params=_cparams(2),
        name="prep_swa",
    )(*args)


def _wkv_kernel(r_ref, k_ref, v_ref, na_ref, nb_ref, lw_ref, cs_ref, g_ref,
                lnw_ref, lnb_ref, rk_ref, ones_ref, o_ref, s_ref):
    c = WKV_CHUNK

    @pl.when(pl.program_id(0) == 0)
    def _():
        s_ref[...] = jnp.zeros_like(s_ref)

    gw = WKV_GROUP * HEAD
    lane = lax.broadcasted_iota(jnp.int32, (1, gw), 1)
    head_masks = [(lane >= h * HEAD) & (lane < (h + 1) * HEAD) for h in range(WKV_GROUP)]
    ti = lax.broadcasted_iota(jnp.int32, (c, gw), 0)
    tj = lax.broadcasted_iota(jnp.int32, (c, gw), 1) & (c - 1)
    strict = ti > tj
    incl = ti >= tj
    eye_t = jnp.where(ti == tj, 1.0, 0.0)
    li = lax.broadcasted_iota(jnp.int32, (gw, gw), 0)
    lj = lax.broadcasted_iota(jnp.int32, (gw, gw), 1)
    bdmask = (li // HEAD) == (lj // HEAD)
    ones_bd = ones_ref[...]

    def bd(x):
        xb = x.astype(BF16)
        zero = jnp.zeros_like(xb)
        return jnp.concatenate([jnp.where(m, xb, zero) for m in head_masks], axis=0)

    n_ck = r_ref.shape[1] // c
    n_state = s_ref.shape[0]
    chains = [(bi, slice(ck * c, (ck + 1) * c), slice(p * gw, (p + 1) * gw))
              for ck in range(n_ck) for bi in range(r_ref.shape[0]) for p in range(A_W // gw)]
    pairs = range(len(chains))
    rd = lambda ref, p: ref[chains[p]]
    rt, at_abs, bh, kh, wc, em, lhs0, rhs0 = [], [], [], [], [], [], [], []
    for p in pairs:
        cs = rd(cs_ref, p)
        mid = cs[c // 2 - 1:c // 2, :]
        last = cs[c - 1:c, :]
        e_neg = jnp.exp(mid - cs)
        rt_p = rd(r_ref, p) * jnp.exp(cs - mid)
        at_p = rd(na_ref, p) * jnp.exp(cs - rd(lw_ref, p) - mid)
        kt_p = rd(k_ref, p) * e_neg
        bt_p = rd(nb_ref, p) * e_neg
        em_p = jnp.exp(mid)
        e_end = jnp.exp(last - mid)
        rt.append(rt_p)
        at_abs.append(at_p * em_p)
        em.append(em_p)
        wc.append(jnp.exp(last))
        bh.append(bt_p * e_end)
        kh.append(kt_p * e_end)
        lhs0.append(jnp.concatenate([at_p, rt_p], axis=0))
        rhs0.append(jnp.concatenate([bd(bt_p), bd(kt_p)], axis=0))
    sc = [_bdot_nt(lhs0[p], rhs0[p]) for p in pairs]
    a_ab = [jnp.where(strict, sc[p][:c, :gw], 0.0) for p in pairs]
    a_rb = [jnp.where(incl, sc[p][c:, :gw], 0.0) for p in pairs]
    akrk = [jnp.concatenate([jnp.where(strict, sc[p][:c, gw:], 0.0),
                             jnp.where(incl, sc[p][c:, gw:], 0.0)], axis=0) for p in pairs]
    xv = [_bdot(akrk[p], bd(rd(v_ref, p))) for p in pairs]
    pinv = [eye_t + a_ab[p] for p in pairs]
    apow = [_bdot(a_ab[p], bd(a_ab[p])) for p in pairs]
    for _ in range(4):
        both = [_bdot(jnp.concatenate([apow[p], pinv[p]], axis=0), bd(apow[p])) for p in pairs]
        pinv = [pinv[p] + both[p][c:] for p in pairs]
        apow = [both[p][:c] for p in pairs]
    pinv = [pinv[p] + _bdot(pinv[p], bd(apow[p])) for p in pairs]
    ta = [_bdot(pinv[p], jnp.concatenate([bd(xv[p][:c]), bd(at_abs[p])], axis=1)) for p in pairs]
    uv = [ta[p][:, :gw] for p in pairs]
    ap = [ta[p][:, gw:] for p in pairs]
    rb = [_bdot(a_rb[p], jnp.concatenate([bd(ap[p]), bd(uv[p])], axis=1)) for p in pairs]
    rpap = [jnp.concatenate([rt[p] * em[p] + rb[p][:, :gw], ap[p]], axis=0) for p in pairs]
    yv = [rb[p][:, gw:] + xv[p][c:] for p in pairs]
    bk = [jnp.concatenate([bh[p], kh[p]], axis=0) for p in pairs]
    state = [s_ref[si] for si in range(n_state)]
    y = []
    for ck in range(n_ck):
        ps = [ck * n_state + si for si in range(n_state)]
        ru = [_bdot_nt(rpap[p], state[si]) for si, p in enumerate(ps)]
        y += [ru[si][:c] + yv[p] for si, p in enumerate(ps)]
        grams = [_bdot_tn(jnp.concatenate([ru[si][c:] + uv[p], rd(v_ref, p)], axis=0), bk[p])
                 for si, p in enumerate(ps)]
        state = [state[si] * wc[p] + jnp.where(bdmask, grams[si], 0.0) for si, p in enumerate(ps)]
    for si in range(n_state):
        s_ref[si] = state[si]
    stat_rows = []
    for p in pairs:
        stat_rows += [y[p], rd(r_ref, p) * rd(k_ref, p) * rk_ref[:, chains[p][2]]]
    stats_all = _bdot(jnp.concatenate(stat_rows, axis=0), ones_bd)
    stats = [stats_all[2 * c * p:2 * c * (p + 1)] for p in pairs]
    d = [y[p] - stats[p][:c] * (1.0 / HEAD) for p in pairs]
    var_all = _bdot(jnp.concatenate([d[p] * d[p] for p in pairs], axis=0), ones_bd) * (1.0 / HEAD)
    var = [var_all[c * p:c * (p + 1)] for p in pairs]
    for p in pairs:
        sl = chains[p][2]
        yn = d[p] * lax.rsqrt(var[p] + LN_X_EPS) * lnw_ref[:, sl] + lnb_ref[:, sl]
        out = yn + stats[p][c:] * rd(v_ref, p)
        o_ref[chains[p]] = out * _silu(rd(g_ref, p))


def _wkv(r, k, v, na, nb, lw, cs, gate, ln_w, ln_b, r_k):
    b, t, _ = r.shape
    c = WKV_CHUNK
    gw = WKV_GROUP * HEAD
    li = jnp.arange(gw)
    ones_bd = (li[:, None] // HEAD == li[None, :] // HEAD).astype(BF16)
    rows = WKV_STEP_CHUNKS * c
    assert t % rows == 0
    tile = pl.BlockSpec((b, rows, A_W), lambda i: (0, i, 0))
    vec = pl.BlockSpec((1, A_W), lambda i: (0, 0))
    return pl.pallas_call(
        _wkv_kernel,
        grid=(t // rows,),
        in_specs=[tile] * 8 + [vec] * 3 + [pl.BlockSpec((gw, gw), lambda i: (0, 0))],
        out_specs=tile,
        out_shape=jax.ShapeDtypeStruct((b, t, A_W), F32),
        scratch_shapes=[pltpu.VMEM((b * (A_W // gw), gw, gw), F32)],
        compiler_params=_cparams(1),
        name="wkv",
    )(r, k, v, na, nb, lw, cs, gate, ln_w.reshape(1, A_W), ln_b.reshape(1, A_W),
      r_k.reshape(1, A_W), ones_bd)


def _rope_tables_kernel(pos_ref, freq_ref, sel_cos_ref, sel_sin_ref, one_ref, cos_ref, sin_ref):
    ang = freq_ref[...] * pos_ref[...].astype(F32)
    for z, sel_ref, base, o_ref in ((jnp.cos(ang), sel_cos_ref, one_ref[...], cos_ref),
                                    (jnp.sin(ang), sel_sin_ref, 0.0, sin_ref)):
        hi = z.astype(BF16).astype(F32)
        mid = (z - hi).astype(BF16).astype(F32)
        lo = (z - hi - mid).astype(BF16).astype(F32)
        pieces = jnp.concatenate([hi, mid, lo], axis=0)
        o_ref[...] = base + lax.dot_general(pieces, sel_ref[...], (((0,), (0,)), ((), ())),
                                            preferred_element_type=F32)


def _rope_tables(positions, cols):
    b, t = positions.shape
    n = b * t
    assert n % cols == 0
    half = ROPE_DIM // 2
    inv_freq = jnp.power(jnp.float32(ROPE_THETA), -jnp.arange(half, dtype=F32) / half)
    li = jnp.arange(LANES) % HEAD
    pick = ((li[None, :] % half) == jnp.arange(half)[:, None]) & (li[None, :] < ROPE_DIM)
    sgn = jnp.where(li < half, -1.0, 1.0)[None, :]
    sel_cos = jnp.tile(pick.astype(F32), (3, 1))
    sel_sin = jnp.tile(pick.astype(F32) * sgn, (3, 1))
    one = (li >= ROPE_DIM).astype(F32).reshape(1, LANES)
    full = lambda shape: pl.BlockSpec(shape, lambda i: (0, 0))
    tab = pl.BlockSpec((cols, LANES), lambda i: (i, 0))
    cos_t, sin_t = pl.pallas_call(
        _rope_tables_kernel,
        grid=(n // cols,),
        in_specs=[pl.BlockSpec((1, cols), lambda i: (0, i)), full((half, 1)),
                  full((3 * half, LANES)), full((3 * half, LANES)), full((1, LANES))],
        out_specs=[tab, tab],
        out_shape=[jax.ShapeDtypeStruct((n, LANES), F32)] * 2,
        compiler_params=_cparams(1),
        name="rope_tables",
    )(positions.reshape(1, n), inv_freq.reshape(half, 1), sel_cos, sel_sin, one)
    return cos_t.reshape(b, t, LANES), sin_t.reshape(b, t, LANES)


def _rope(x, cos_t, sin_t, first8):
    fwd = pltpu.roll(x, LANES - ROPE_DIM // 2, axis=1)
    bwd = pltpu.roll(x, ROPE_DIM // 2, axis=1)
    return x * cos_t + jnp.where(first8, fwd, bwd) * sin_t


def _lane_tile(j):
    return slice(j * LANES, (j + 1) * LANES)


def _swa_consts():
    blk = WINDOW
    m0, m1 = _head_masks()
    lane = lax.broadcasted_iota(jnp.int32, (1, LANES), 1)
    first8 = (lane & (HEAD - 1)) < ROPE_DIM // 2
    ki = lax.broadcasted_iota(jnp.int32, (2 * blk, blk), 0)
    qi = lax.broadcasted_iota(jnp.int32, (2 * blk, blk), 1) + blk
    band = (ki <= qi) & (qi - ki < WINDOW)
    in_cur = ki >= blk
    return m0, m1, first8, band, in_cur


def _swa_scores(q_tile_at, k_win, v_win, cos_c, sin_c, consts):
    m0, m1, first8, _, _ = consts
    scale = 1.0 / math.sqrt(HEAD)
    values_t = [v_win[kt].T for kt in range(B_KVW // LANES)]
    scores_t = []
    for g in range(B_KVW // HEAD):
        k_tile = k_win[g // 2]
        swapped = pltpu.roll(k_tile, HEAD, axis=1)
        k_dup = jnp.where(m0, k_tile, swapped) if g % 2 == 0 else jnp.where(m0, swapped, k_tile)
        rows = []
        for pr in range(B_GROUP // 2):
            qp = _rope(q_tile_at(2 * g + pr), cos_c, sin_c, first8) * scale
            rows += [jnp.where(m0, qp, 0.0), jnp.where(m1, qp, 0.0)]
        scores_t.append(_bdot_nt(k_dup, jnp.concatenate(rows, axis=0)))
    return scores_t, values_t


def _swa_outputs(scores_t, values_t, has_prev, sink_ref, gate_tile_at, store_tile, consts):
    _, _, _, band, in_cur = consts
    blk = WINDOW
    valid = band & (has_prev | in_cur)
    ones_rows = jnp.ones((16, 2 * blk), BF16)
    for g in range(B_KVW // HEAD):
        v_t = values_t[g // 2][(g % 2) * HEAD:(g % 2 + 1) * HEAD]
        lhs = jnp.concatenate([v_t.astype(BF16), ones_rows], axis=0)
        for pr in range(B_GROUP // 2):
            qt = 2 * g + pr
            halves = []
            for half in range(2):
                c0 = (2 * pr + half) * blk
                s = jnp.where(valid, scores_t[g][:, c0:c0 + blk], -jnp.inf)
                sink = sink_ref[:, qt * LANES + half * HEAD:qt * LANES + half * HEAD + 1]
                m = jnp.maximum(jnp.max(s, axis=0, keepdims=True), sink)
                p = jnp.exp(s - m).astype(BF16)
                od = jnp.dot(lhs, p, preferred_element_type=F32)
                den = od[HEAD:HEAD + 1] + jnp.exp(sink - m)
                halves.append(od[:HEAD] * (1.0 / den))
            o_tile = jnp.concatenate(halves, axis=0).T
            store_tile(qt, o_tile * _silu(gate_tile_at(qt)))


def _out_proj_kernel(x_ref, ya_ref, yb_ref, w_ref, o_ref):
    y = jnp.concatenate([ya_ref[...].astype(BF16), yb_ref[...].astype(BF16)], axis=1)
    o_ref[...] = x_ref[...] + jnp.dot(y, w_ref[...], preferred_element_type=F32)


def _out_proj(x2d, ya, yb, w_stack, layer, tm):
    n, d = x2d.shape
    return pl.pallas_call(
        _out_proj_kernel,
        grid=(n // tm,),
        in_specs=[
            pl.BlockSpec((tm, d), lambda i: (i, 0)),
            pl.BlockSpec((tm, A_W), lambda i: (i, 0)),
            pl.BlockSpec((tm, B_W), lambda i: (i, 0)),
            pl.BlockSpec((None, A_W + B_W, d), lambda i: (layer, 0, 0), pipeline_mode=pl.Buffered(1)),
        ],
        out_specs=pl.BlockSpec((tm, d), lambda i: (i, 0)),
        out_shape=jax.ShapeDtypeStruct((n, d), F32),
        compiler_params=_cparams(1),
        name="out_proj",
    )(x2d, ya, yb, w_stack)


def _odd_layer_kernel(x_ref, g_ref, win_ref, lnw_ref, lnb_ref, ws_ref, bst_ref, wout_ref,
                      fin_ref, o_ref, y_ref, *, final_norm):
    tm, d = x_ref.shape
    ch = SGU_CHUNK
    x = x_ref[...]
    ms = jnp.mean(x * x, axis=-1, keepdims=True)
    h = ((x * lax.rsqrt(ms + RMS_EPS)) * g_ref[...]).astype(BF16)
    v = jnp.dot(h, win_ref[:, d:2 * d], preferred_element_type=F32)
    mean = jnp.mean(v, axis=-1, keepdims=True)
    dv = v - mean
    var = jnp.mean(dv * dv, axis=-1, keepdims=True)
    vn = ((dv * lax.rsqrt(var + LN_EPS)) * lnw_ref[...] + lnb_ref[...]).astype(BF16)
    ti = lax.broadcasted_iota(jnp.int32, (ch, ch), 0)
    si = lax.broadcasted_iota(jnp.int32, (ch, ch), 1)
    causal = ti >= si
    cw = 2 * LANES
    zero_tile = jnp.zeros((ch, LANES), BF16)
    for j in range(d // cw):
        c0 = j * cw
        u = jnp.dot(h, win_ref[:, c0:c0 + cw], preferred_element_type=F32)
        gate = jnp.dot(h, win_ref[:, 2 * d + c0:2 * d + c0 + cw], preferred_element_type=F32)
        wm = jnp.concatenate(
            [jnp.where(causal, ws_ref[2 * j + gi], 0.0).astype(BF16) for gi in range(2)], axis=1)
        bias = jnp.concatenate(
            [jnp.broadcast_to(bst_ref[:, 2 * j + gi:2 * j + gi + 1], (ch, LANES)) for gi in range(2)],
            axis=1)
        mixed = []
        for ci in range(tm // ch):
            vc = vn[ci * ch:(ci + 1) * ch, c0:c0 + cw]
            v_bd = jnp.concatenate(
                [jnp.concatenate([vc[:, :LANES], zero_tile], axis=1),
                 jnp.concatenate([zero_tile, vc[:, LANES:]], axis=1)], axis=0)
            mixed.append(jnp.dot(wm, v_bd, preferred_element_type=F32) + bias)
        y_ref[:, c0:c0 + cw] = (u * jnp.concatenate(mixed, axis=0) * _silu(gate)).astype(BF16)
    out = x + jnp.dot(y_ref[...], wout_ref[...], preferred_element_type=F32)
    if final_norm:
        ms = jnp.mean(out * out, axis=-1, keepdims=True)
        out = (out * lax.rsqrt(ms + RMS_EPS)) * fin_ref[...]
    o_ref[...] = out


def _odd_layer(x2d, g, w_in_stack, ln_w, ln_b, ws, bs, w_out_stack, layer, final_g, tm):
    n, d = x2d.shape
    final_norm = final_g is not None
    fin = final_g if final_norm else jnp.ones((d,), F32)
    row_tile = pl.BlockSpec((tm, d), lambda i: (i, 0))
    vec = pl.BlockSpec((1, d), lambda i: (0, 0))
    return pl.pallas_call(
        functools.partial(_odd_layer_kernel, final_norm=final_norm),
        grid=(n // tm,),
        in_specs=[
            row_tile, vec,
            pl.BlockSpec((None, d, 3 * d), lambda i: (layer, 0, 0), pipeline_mode=pl.Buffered(1)),
            vec, vec,
            pl.BlockSpec((SGU_GROUPS, SGU_CHUNK, SGU_CHUNK), lambda i: (0, 0, 0)),
            pl.BlockSpec((SGU_CHUNK, SGU_GROUPS), lambda i: (0, 0)),
            pl.BlockSpec((None, d, d), lambda i: (layer, 0, 0), pipeline_mode=pl.Buffered(1)),
            vec,
        ],
        out_specs=row_tile,
        out_shape=jax.ShapeDtypeStruct((n, d), F32),
        scratch_shapes=[pltpu.VMEM((tm, d), BF16)],
        compiler_params=_cparams(1),
        name="odd_layer",
    )(x2d, g.reshape(1, d), w_in_stack, ln_w.reshape(1, d), ln_b.reshape(1, d), ws, bs.T,
      w_out_stack, fin.reshape(1, d))


def kernel(x, positions, e_norm, e_w_in, e_mu, rwkv_w0, rwkv_w2, rwkv_a0, rwkv_a2, rwkv_k_k, rwkv_k_a, rwkv_r_k, rwkv_ln_w, rwkv_ln_b, rwkv_v0, rwkv_v1, rwkv_v2, attn_sinks, e_w_out, o_norm, o_w_in, sgu_ln_w, sgu_ln_b, sgu_ws, sgu_bs, o_w_out, final_norm):
    b, t, d = x.shape
    n = b * t
    depth = e_norm.shape[0] + o_norm.shape[0]
    assert t % 256 == 0 and d == D_MODEL and depth % 2 == 0
    x2d = x.reshape(n, d)
    cos_t, sin_t = _rope_tables(positions, cols=2048)
    e_w_in_b, e_w_out_b = e_w_in.astype(BF16), e_w_out.astype(BF16)
    o_w_in_b, o_w_out_b = o_w_in.astype(BF16), o_w_out.astype(BF16)
    v_first = None
    for layer in range(depth):
        if layer % 2 == 0:
            e = layer // 2
            w_rest = e_w_in_b[e, :, SHIFT_W:]
            r3 = lambda z: z.reshape(b, t, z.shape[-1])
            vres = None if e == 0 else (v_first, rwkv_v0[e - 1], rwkv_v1[e - 1], rwkv_v2[e - 1])
            a_gate, q, kv, b_gate = _norm_matmul(
                x2d, e_norm[e], w_rest, (A_W, B_W, 2 * B_KVW, B_W), tm=512)
            r, k2, v, na, nb, lw, cs, yb = _prep_swa(
                r3(x2d), e_norm[e], e_w_in_b, e, e_mu[e], rwkv_w0[e], rwkv_w2[e], rwkv_a0[e],
                rwkv_a2[e], rwkv_k_k[e], rwkv_k_a[e], vres, r3(q), r3(kv), cos_t, sin_t, r3(b_gate),
                attn_sinks[e], tt=256)
            if e == 0:
                v_first = v
            ya = _wkv(r, k2, v, na, nb, lw, cs, r3(a_gate), rwkv_ln_w[e], rwkv_ln_b[e], rwkv_r_k[e])
            x2d = _out_proj(x2d, ya.reshape(n, A_W), yb.reshape(n, B_W), e_w_out_b, e, tm=512)
        else:
            o = layer // 2
            fin = final_norm if layer == depth - 1 else None
            x2d = _odd_layer(x2d, o_norm[o], o_w_in_b, sgu_ln_w[o], sgu_ln_b[o],
                             sgu_ws[o], sgu_bs[o], o_w_out_b, o, fin, tm=256)
    return x2d.reshape(b, t, d)
```
